```python
import math
import jax, jax.numpy as jnp
from jax import lax
import numpy as np

D_MODEL = 4096
BATCH = 32
SEQ = 256
DEPTH = 4
DEC_BATCH = 2
DEC_SEQ = 4096
PAST_LEN = 512

GRID_W = 64
N_MIXERS = 3
N_ATTN = (DEPTH + 2) // N_MIXERS
N_SSD = (DEPTH + 1) // N_MIXERS
N_HGRN = DEPTH // N_MIXERS
ATTN_HEADS = 32
ATTN_KV_HEADS = 8
ATTN_GROUP = ATTN_HEADS // ATTN_KV_HEADS
HEAD_DIM = D_MODEL // ATTN_HEADS
AXIS_DIM = HEAD_DIM // 2
ATTN_WINDOW = 128
ATTN_BLOCK = 128
ATTN_SCALE = HEAD_DIM ** -0.5
ROPE_THETA = 10000.0
ATTN_IN = ATTN_HEADS * HEAD_DIM + 2 * ATTN_KV_HEADS * HEAD_DIM
SSD_INNER = 2 * D_MODEL
SSD_HEAD_DIM = 64
SSD_HEADS = SSD_INNER // SSD_HEAD_DIM
SSD_GROUPS = 8
SSD_STATE = 128
SSD_GN = SSD_GROUPS * SSD_STATE
SSD_CONV = 5
SSD_CONV_CH = SSD_INNER + 2 * SSD_GN
SSD_CHUNK = 64
SSD_IN = SSD_INNER + SSD_CONV_CH + 2 * SSD_HEADS
HGRN_DK = 128
HGRN_HEADS = D_MODEL // HGRN_DK
HGRN_DV = D_MODEL // HGRN_HEADS
HGRN_HK = HGRN_HEADS * HGRN_DK
HGRN_HV = HGRN_HEADS * HGRN_DV
HGRN_CHUNK = 64
HGRN_IN = 3 * HGRN_HK + 2 * HGRN_HV
D_FF = 11008
FFN_RES_W = 0.5
N_MOD = 9
EPS = 1e-6
NEG_INF = -1e30

kernel_name = "hybrid_diffusion_attn_ssd_hgrn2_step"

f32 = jnp.float32


def rmsnorm(x, g):
    xf = x.astype(f32)
    y = xf * lax.rsqrt(jnp.mean(xf * xf, axis=-1, keepdims=True) + EPS)
    return (y * g.astype(f32)).astype(x.dtype)


def modulation(cond, w, b):
    m = jax.nn.silu(cond) @ w + b
    return m.reshape(cond.shape[:-1] + (N_MOD, D_MODEL))


def ada_in(x, mod, j, g):
    return rmsnorm(x, g) * (1 + mod[:, 3 * j + 1]) + mod[:, 3 * j]


def ada_out(x, y, mod, j, g, res_w):
    return x + res_w * mod[:, 3 * j + 2] * rmsnorm(y, g)


def swiglu(h, w_gu, w_down):
    gu = h @ w_gu
    return (jax.nn.silu(gu[..., :D_FF]) * gu[..., D_FF:]) @ w_down


def axial_rope_tables(n_tokens):
    rows = n_tokens // GRID_W
    r = jnp.repeat(jnp.arange(rows), GRID_W).astype(f32)
    col = jnp.tile(jnp.arange(GRID_W), rows).astype(f32)
    inv = ROPE_THETA ** (-jnp.arange(0, AXIS_DIM, 2, dtype=f32) / AXIS_DIM)
    ang = jnp.stack([r[:, None] * inv, col[:, None] * inv], axis=1)
    return jnp.cos(ang), jnp.sin(ang)


def apply_axial_rope(x, cos, sin):
    B, L, H, _ = x.shape
    xf = x.astype(f32).reshape(B, L, H, 2, 2, AXIS_DIM // 2)
    x1, x2 = xf[..., 0, :], xf[..., 1, :]
    c, s = cos[None, :, None], sin[None, :, None]
    out = jnp.stack([x1 * c - x2 * s, x2 * c + x1 * s], axis=-2)
    return out.reshape(B, L, H, HEAD_DIM).astype(x.dtype)


def attn_qkv(h, w_in):
    B, L, _ = h.shape
    qkv = h @ w_in
    nq, nk = ATTN_HEADS * HEAD_DIM, ATTN_KV_HEADS * HEAD_DIM
    q = qkv[..., :nq].reshape(B, L, ATTN_HEADS, HEAD_DIM)
    k = qkv[..., nq:nq + nk].reshape(B, L, ATTN_KV_HEADS, HEAD_DIM)
    v = qkv[..., nq + nk:].reshape(B, L, ATTN_KV_HEADS, HEAD_DIM)
    return q, k, v


def block_attend(q, k, v, bias, sink):
    s = jnp.einsum("bqhgd,bkhd->bhgqk", q, k).astype(f32) * ATTN_SCALE + bias
    sink_col = jnp.broadcast_to(sink.astype(f32).reshape(ATTN_KV_HEADS, ATTN_GROUP)[None, :, :, None, None], s.shape[:-1] + (1,))
    p = jax.nn.softmax(jnp.concatenate([s, sink_col], axis=-1), axis=-1)[..., :-1]
    return jnp.einsum("bhgqk,bkhd->bqhgd", p.astype(v.dtype), v)


def attn_context(h, w_in, sink, w_out):
    B, L, _ = h.shape
    q, k, v = attn_qkv(h, w_in)
    nb = L // ATTN_BLOCK
    qb = q.reshape(B, nb, ATTN_BLOCK, ATTN_KV_HEADS, ATTN_GROUP, HEAD_DIM).swapaxes(0, 1)
    o = lax.map(lambda qi: block_attend(qi, k, v, 0.0, sink), qb)
    return o.swapaxes(0, 1).reshape(B, L, ATTN_HEADS * HEAD_DIM) @ w_out, k, v


def attn_latent(h, k_ctx, v_ctx, cos, sin, w_in, sink, w_out):
    B, L, _ = h.shape
    q, k, v = attn_qkv(h, w_in)
    q = apply_axial_rope(q, cos, sin).reshape(B, L, ATTN_KV_HEADS, ATTN_GROUP, HEAD_DIM)
    k = apply_axial_rope(k, cos, sin)
    nb = L // ATTN_BLOCK
    band = ATTN_BLOCK + 2 * ATTN_WINDOW
    start = jnp.arange(nb) * ATTN_BLOCK
    kidx = start[:, None] + jnp.arange(band)[None, :]
    kpos = kidx - ATTN_WINDOW
    qpos = start[:, None] + jnp.arange(ATTN_BLOCK)[None, :]
    valid = (jnp.abs(qpos[:, :, None] - kpos[:, None, :]) <= ATTN_WINDOW) & ((kpos >= 0) & (kpos < L))[:, None, :]
    bias_loc = jnp.where(valid, 0.0, NEG_INF).astype(f32)
    pad = ((0, 0), (ATTN_WINDOW, ATTN_WINDOW), (0, 0), (0, 0))
    kb = jnp.pad(k, pad)[:, kidx].swapaxes(0, 1)
    vb = jnp.pad(v, pad)[:, kidx].swapaxes(0, 1)
    qb = q.reshape(B, nb, ATTN_BLOCK, ATTN_KV_HEADS, ATTN_GROUP, HEAD_DIM).swapaxes(0, 1)
    ctx_bias = jnp.zeros((ATTN_BLOCK, k_ctx.shape[1]), f32)
    k_ctx = k_ctx.astype(k.dtype)
    v_ctx = v_ctx.astype(v.dtype)

    def one_block(blk):
        qi, ki, vi, bi = blk
        return block_attend(qi, jnp.concatenate([ki, k_ctx], axis=1), jnp.concatenate([vi, v_ctx], axis=1),
                            jnp.concatenate([bi, ctx_bias], axis=-1), sink)

    o = lax.map(one_block, (qb, kb, vb, bias_loc))
    return o.swapaxes(0, 1).reshape(B, L, ATTN_HEADS * HEAD_DIM) @ w_out


def centred_dwconv(x, w, b):
    y = lax.conv_general_dilated(x, w[:, None, :].astype(x.dtype), window_strides=(1,),
                                 padding=[(SSD_CONV // 2, SSD_CONV // 2)],
                                 dimension_numbers=("NWC", "WIO", "NWC"),
                                 feature_group_count=x.shape[-1])
    return y + b.astype(x.dtype)


def ssd_chunk_scan(x, dt, A, Bm, Cm, h0):
    Bsz, L, H, P = x.shape
    G, N = Bm.shape[2], Bm.shape[3]
    R = H // G
    Q = SSD_CHUNK
    nc = L // Q
    x = x.reshape(Bsz, nc, Q, G, R, P)
    dt = dt.reshape(Bsz, nc, Q, G, R)
    Bm = Bm.reshape(Bsz, nc, Q, G, N)
    Cm = Cm.reshape(Bsz, nc, Q, G, N)
    acum = jnp.cumsum(dt * A.reshape(G, R), axis=2)
    causal = jnp.tril(jnp.ones((Q, Q), bool))[:, :, None, None]
    seg = acum[:, :, :, None] - acum[:, :, None, :]
    decay = jnp.exp(jnp.where(causal, seg, -jnp.inf))
    cb = jnp.einsum("bcign,bcjgn->bcijg", Cm, Bm)
    w = cb[..., None] * decay * dt[:, :, None]
    y = jnp.einsum("bcijgr,bcjgrp->bcigrp", w, x)
    to_end = jnp.exp(acum[:, :, -1:] - acum) * dt
    s = jnp.einsum("bcjgr,bcjgn,bcjgrp->bcgrpn", to_end, Bm, x)
    chunk_decay = jnp.exp(acum[:, :, -1])

    def step(hs, inp):
        d, sc = inp
        return hs * d[..., None, None] + sc, hs

    h_last, h_prev = lax.scan(step, h0.reshape(Bsz, G, R, P, N), (chunk_decay.swapaxes(0, 1), s.swapaxes(0, 1)))
    h_prev = h_prev.swapaxes(0, 1)
    y = y + jnp.einsum("bcign,bcgrpn,bcigr->bcigrp", Cm, h_prev, jnp.exp(acum))
    return y.reshape(Bsz, L, H, P), h_last.reshape(Bsz, H, P, N)


def ssd_mixer(h, h0, w_in, conv_w, conv_b, dt_bias, A_log, D_skip, norm_g, w_out):
    B, L, _ = h.shape
    proj = h @ w_in
    z = proj[..., :SSD_INNER]
    xbc = jax.nn.silu(centred_dwconv(proj[..., SSD_INNER:SSD_INNER + SSD_CONV_CH], conv_w, conv_b)).astype(f32)
    dt_raw = proj[..., SSD_INNER + SSD_CONV_CH:].reshape(B, L, 2, SSD_HEADS).astype(f32)
    x = xbc[..., :SSD_INNER].reshape(B, L, SSD_HEADS, SSD_HEAD_DIM)
    Bm = xbc[..., SSD_INNER:SSD_INNER + SSD_GN].reshape(B, L, SSD_GROUPS, SSD_STATE)
    Cm = xbc[..., SSD_INNER + SSD_GN:].reshape(B, L, SSD_GROUPS, SSD_STATE)
    dt = jax.nn.softplus(dt_raw + dt_bias.astype(f32))
    A = -jnp.exp(A_log.astype(f32))
    h0 = h0.astype(f32)
    flip = lambda t: jnp.flip(t, axis=1)
    y_f, s_f = ssd_chunk_scan(x, dt[:, :, 0], A[0], Bm, Cm, h0[:, 0])
    y_b, s_b = ssd_chunk_scan(flip(x), flip(dt[:, :, 1]), A[1], flip(Bm), flip(Cm), h0[:, 1])
    y = y_f + flip(y_b) + D_skip.astype(f32)[:, None] * x
    y = y.reshape(B, L, SSD_INNER) * jax.nn.silu(z.astype(f32))
    yg = y.reshape(B, L, SSD_GROUPS, SSD_INNER // SSD_GROUPS)
    yg = yg * lax.rsqrt(jnp.mean(yg * yg, axis=-1, keepdims=True) + EPS)
    y = yg.reshape(B, L, SSD_INNER) * norm_g.astype(f32)
    return y.astype(h.dtype) @ w_out, jnp.stack([s_f, s_b], axis=1)


def hgrn_chunk_scan(q, k, v, logf, S0):
    Bsz, L, H, DK = q.shape
    DV = v.shape[-1]
    C = HGRN_CHUNK
    nc = L // C
    q = q.reshape(Bsz, nc, C, H, DK)
    k = k.reshape(Bsz, nc, C, H, DK)
    v = v.reshape(Bsz, nc, C, H, DV)
    b = jnp.cumsum(logf.reshape(Bsz, nc, C, H, DK), axis=2)
    qd = q * jnp.exp(b)
    kd = k * jnp.exp(-b)
    causal = jnp.tril(jnp.ones((C, C), bool))
    att = jnp.where(causal, jnp.einsum("bcihk,bcjhk->bchij", qd, kd), 0.0)
    o = jnp.einsum("bchij,bcjhv->bcihv", att, v)
    s = jnp.einsum("bcjhk,bcjhv->bchkv", k * jnp.exp(b[:, :, -1:] - b), v)
    dec = jnp.exp(b[:, :, -1])

    def step(S, inp):
        d, sc = inp
        return d[..., None] * S + sc, S

    S_last, S_prev = lax.scan(step, S0, (dec.swapaxes(0, 1), s.swapaxes(0, 1)))
    o = o + jnp.einsum("bcihk,bchkv->bcihv", qd, S_prev.swapaxes(0, 1))
    return o.reshape(Bsz, L, H, DV), S_last


def hgrn_mixer(h, S0, lb, w_in, norm_g, w_out):
    B, L, _ = h.shape
    proj = h @ w_in
    o1, o2, o3, o4 = HGRN_HK, HGRN_HK + HGRN_HV, 2 * HGRN_HK + HGRN_HV, 3 * HGRN_HK + HGRN_HV
    q = jax.nn.silu(proj[..., :o1].astype(f32)).reshape(B, L, HGRN_HEADS, HGRN_DK)
    i = proj[..., o1:o2].astype(f32).reshape(B, L, HGRN_HEADS, HGRN_DV)
    g = proj[..., o4:].astype(f32).reshape(B, L, HGRN_HEADS, HGRN_DV)

    def gates(f_raw, lbd):
        sig = jax.nn.sigmoid(f_raw.astype(f32))
        f = lbd + (1 - lbd) * sig
        k = (1 - lbd) * jax.nn.sigmoid(-f_raw.astype(f32))
        return k.reshape(B, L, HGRN_HEADS, HGRN_DK), jnp.log(f).reshape(B, L, HGRN_HEADS, HGRN_DK)

    k_f, lf_f = gates(proj[..., o2:o3], lb[0])
    k_b, lf_b = gates(proj[..., o3:o4], lb[1])
    S0 = S0.astype(f32)
    flip = lambda t: jnp.flip(t, axis=1)
    y_f, S_f = hgrn_chunk_scan(q, k_f, i, lf_f, S0[:, 0])
    y_b, S_b = hgrn_chunk_scan(flip(q), flip(k_b), flip(i), flip(lf_b), S0[:, 1])
    o = y_f + flip(y_b)
    o = o * lax.rsqrt(jnp.mean(o * o, axis=-1, keepdims=True) + EPS) * norm_g.astype(f32) * jax.nn.silu(g)
    return o.reshape(B, L, HGRN_HV).astype(h.dtype) @ w_out, jnp.stack([S_f, S_b], axis=1)


def setup_inputs(seed: int = 0) -> dict:
    key = jax.random.key(seed)
    ks = iter(jax.random.split(key, 40))
    nrm = lambda shape, scale: jax.random.normal(next(ks), shape, f32) * scale
    uni = lambda shape, lo, hi: jax.random.uniform(next(ks), shape, f32, lo, hi)
    x_prompt = nrm((BATCH, SEQ, D_MODEL), 1.0)
    x_sample = nrm((DEC_BATCH, DEC_SEQ, D_MODEL), 1.0)
    cache_attn_k = nrm((DEC_BATCH, N_ATTN, PAST_LEN, ATTN_KV_HEADS, HEAD_DIM), 1.0)
    cache_attn_v = nrm((DEC_BATCH, N_ATTN, PAST_LEN, ATTN_KV_HEADS, HEAD_DIM), 1.0)
    state_ssd = nrm((DEC_BATCH, N_SSD, 2, SSD_HEADS, SSD_HEAD_DIM, SSD_STATE), 0.5)
    state_hgrn = nrm((DEC_BATCH, N_HGRN, 2, HGRN_HEADS, HGRN_DK, HGRN_DV), 0.5)
    c = nrm((DEC_BATCH, D_MODEL), 1.0)
    c_ctx = nrm((D_MODEL,), 1.0)
    mod_w = nrm((DEPTH, D_MODEL, N_MOD * D_MODEL), D_MODEL ** -0.5)
    mod_b = nrm((DEPTH, N_MOD * D_MODEL), 0.02)
    norm_g = 1.0 + nrm((DEPTH, 6, D_MODEL), 0.02)
    ffn_w_gu = nrm((DEPTH, 2, D_MODEL, 2 * D_FF), D_MODEL ** -0.5)
    ffn_w_down = nrm((DEPTH, 2, D_FF, D_MODEL), D_FF ** -0.5)
    attn_w_in = nrm((N_ATTN, D_MODEL, ATTN_IN), D_MODEL ** -0.5)
    attn_sink = nrm((N_ATTN, ATTN_HEADS), 0.5)
    attn_w_out = nrm((N_ATTN, ATTN_HEADS * HEAD_DIM, D_MODEL), (ATTN_HEADS * HEAD_DIM) ** -0.5)
    ssd_w_in = nrm((N_SSD, D_MODEL, SSD_IN), D_MODEL ** -0.5)
    ssd_conv_w = nrm((N_SSD, SSD_CONV, SSD_CONV_CH), SSD_CONV ** -0.5)
    ssd_conv_b = nrm((N_SSD, SSD_CONV_CH), 0.02)
    dt0 = jnp.exp(uni((N_SSD, 2, SSD_HEADS), math.log(1e-3), math.log(1e-1)))
    ssd_dt_bias = dt0 + jnp.log(-jnp.expm1(-dt0))
    ssd_A_log = jnp.log(uni((N_SSD, 2, SSD_HEADS), 1.0, 16.0))
    ssd_D = 1.0 + nrm((N_SSD, SSD_HEADS), 0.1)
    ssd_norm_g = 1.0 + nrm((N_SSD, SSD_INNER), 0.02)
    ssd_w_out = nrm((N_SSD, SSD_INNER, D_MODEL), SSD_INNER ** -0.5)
    hgrn_w_in = nrm((N_HGRN, D_MODEL, HGRN_IN), D_MODEL ** -0.5)
    hgrn_lb_logits = nrm((DEPTH, 2, HGRN_HK), 0.1)
    hgrn_norm_g = 1.0 + nrm((N_HGRN, HGRN_DV), 0.02)
    hgrn_w_out = nrm((N_HGRN, HGRN_HV, D_MODEL), HGRN_HV ** -0.5)
    return {"x_prompt": x_prompt, "x_sample": x_sample, "cache_attn_k": cache_attn_k, "cache_attn_v": cache_attn_v,
            "state_ssd": state_ssd, "state_hgrn": state_hgrn, "c": c, "c_ctx": c_ctx,
            "mod_w": mod_w, "mod_b": mod_b, "norm_g": norm_g, "ffn_w_gu": ffn_w_gu, "ffn_w_down": ffn_w_down,
            "attn_w_in": attn_w_in, "attn_sink": attn_sink, "attn_w_out": attn_w_out,
            "ssd_w_in": ssd_w_in, "ssd_conv_w": ssd_conv_w, "ssd_conv_b": ssd_conv_b, "ssd_dt_bias": ssd_dt_bias,
            "ssd_A_log": ssd_A_log, "ssd_D": ssd_D, "ssd_norm_g": ssd_norm_g, "ssd_w_out": ssd_w_out,
            "hgrn_w_in": hgrn_w_in, "hgrn_lb_logits": hgrn_lb_logits, "hgrn_norm_g": hgrn_norm_g, "hgrn_w_out": hgrn_w_out}


def reference(x_prompt, x_sample, cache_attn_k, cache_attn_v, state_ssd, state_hgrn, c, c_ctx,
              mod_w, mod_b, norm_g, ffn_w_gu, ffn_w_down,
              attn_w_in, attn_sink, attn_w_out,
              ssd_w_in, ssd_conv_w, ssd_conv_b, ssd_dt_bias, ssd_A_log, ssd_D, ssd_norm_g, ssd_w_out,
              hgrn_w_in, hgrn_lb_logits, hgrn_norm_g, hgrn_w_out):
    xc, xl = x_prompt, x_sample
    Bc = xc.shape[0]
    cos, sin = axial_rope_tables(xl.shape[1])
    lb_all = jnp.cumsum(jax.nn.softmax(hgrn_lb_logits.astype(f32), axis=0), axis=0)
    lb_all = lb_all - lb_all[0:1]
    new_k, new_v, new_ssd, new_hgrn = [], [], [], []
    for i in range(DEPTH):
        mc = modulation(c_ctx, mod_w[i], mod_b[i])[None, :, None, :]
        ml = modulation(c, mod_w[i], mod_b[i])[:, :, None, :]
        xc = ada_out(xc, swiglu(ada_in(xc, mc, 0, norm_g[i, 0]), ffn_w_gu[i, 0], ffn_w_down[i, 0]), mc, 0, norm_g[i, 1], FFN_RES_W)
        xl = ada_out(xl, swiglu(ada_in(xl, ml, 0, norm_g[i, 0]), ffn_w_gu[i, 0], ffn_w_down[i, 0]), ml, 0, norm_g[i, 1], FFN_RES_W)
        hc = ada_in(xc, mc, 1, norm_g[i, 2])
        hl = ada_in(xl, ml, 1, norm_g[i, 2])
        kind, slot = i % N_MIXERS, i // N_MIXERS
        if kind == 0:
            yc, k_c, v_c = attn_context(hc, attn_w_in[slot], attn_sink[slot], attn_w_out[slot])
            new_k.append(k_c)
            new_v.append(v_c)
            yl = attn_latent(hl, cache_attn_k[:, slot], cache_attn_v[:, slot], cos, sin,
                             attn_w_in[slot], attn_sink[slot], attn_w_out[slot])
        elif kind == 1:
            ssd_args = (ssd_w_in[slot], ssd_conv_w[slot], ssd_conv_b[slot], ssd_dt_bias[slot], ssd_A_log[slot],
                        ssd_D[slot], ssd_norm_g[slot], ssd_w_out[slot])
            h0 = jnp.zeros((Bc, 2, SSD_HEADS, SSD_HEAD_DIM, SSD_STATE), f32)
            yc, s_c = ssd_mixer(hc, h0, *ssd_args)
            new_ssd.append(s_c)
            yl, _ = ssd_mixer(hl, state_ssd[:, slot], *ssd_args)
        else:
            S0 = jnp.zeros((Bc, 2, HGRN_HEADS, HGRN_DK, HGRN_DV), f32)
            yc, s_c = hgrn_mixer(hc, S0, lb_all[i], hgrn_w_in[slot], hgrn_norm_g[slot], hgrn_w_out[slot])
            new_hgrn.append(s_c)
            yl, _ = hgrn_mixer(hl, state_hgrn[:, slot], lb_all[i], hgrn_w_in[slot], hgrn_norm_g[slot], hgrn_w_out[slot])
        xc = ada_out(xc, yc, mc, 1, norm_g[i, 3], 1.0)
        xl = ada_out(xl, yl, ml, 1, norm_g[i, 3], 1.0)
        xc = ada_out(xc, swiglu(ada_in(xc, mc, 2, norm_g[i, 4]), ffn_w_gu[i, 1], ffn_w_down[i, 1]), mc, 2, norm_g[i, 5], FFN_RES_W)
        xl = ada_out(xl, swiglu(ada_in(xl, ml, 2, norm_g[i, 4]), ffn_w_gu[i, 1], ffn_w_down[i, 1]), ml, 2, norm_g[i, 5], FFN_RES_W)
    new_attn_k = jnp.stack(new_k, axis=1)
    new_attn_v = jnp.stack(new_v, axis=1)
    new_ssd_state = jnp.stack(new_ssd, axis=1).astype(state_ssd.dtype)
    new_hgrn_state = jnp.stack(new_hgrn, axis=1).astype(state_hgrn.dtype)
    return (xc, xl, new_attn_k, new_attn_v, new_ssd_state, new_hgrn_state)
```

```python
import functools
import math
from typing import NamedTuple

import jax
import jax.numpy as jnp
from jax import lax
from jax.experimental import pallas as pl
from jax.experimental.pallas import tpu as pltpu

F32 = jnp.float32
BF16 = jnp.bfloat16

D_MODEL = 4096
DEPTH = 4
GRID_W = 64
N_MIXERS = 3
ATTN_HEADS = 32
ATTN_KV_HEADS = 8
ATTN_GROUP = ATTN_HEADS // ATTN_KV_HEADS
HEAD_DIM = D_MODEL // ATTN_HEADS
AXIS_DIM = HEAD_DIM // 2
ATTN_WINDOW = 128
ATTN_SCALE = HEAD_DIM ** -0.5
ROPE_THETA = 10000.0
ATTN_NQ = ATTN_HEADS * HEAD_DIM
ATTN_NK = ATTN_KV_HEADS * HEAD_DIM
ATTN_IN = ATTN_NQ + 2 * ATTN_NK
SSD_INNER = 2 * D_MODEL
SSD_HEAD_DIM = 64
SSD_HEADS = SSD_INNER // SSD_HEAD_DIM
SSD_GROUPS = 8
SSD_GROUP_HEADS = SSD_HEADS // SSD_GROUPS
SSD_STATE = 128
SSD_GN = SSD_GROUPS * SSD_STATE
SSD_CONV = 5
SSD_CONV_CH = SSD_INNER + 2 * SSD_GN
SSD_CHUNK = 64
SSD_IN = SSD_INNER + SSD_CONV_CH + 2 * SSD_HEADS
HGRN_DK = 128
HGRN_HEADS = D_MODEL // HGRN_DK
HGRN_DV = D_MODEL // HGRN_HEADS
HGRN_HK = HGRN_HEADS * HGRN_DK
HGRN_HV = HGRN_HEADS * HGRN_DV
HGRN_CHUNK = 64
HGRN_IN = 3 * HGRN_HK + 2 * HGRN_HV
D_FF = 11008
FFN_RES_W = 0.5
N_MOD = 9
EPS = 1e-6
NEG_INF = -1e30

V7X_VMEM_BYTES = 64 * 1024 * 1024
V7X_SUBLANES = 8
V7X_LANES = 128
VMEM_LIMIT_CAP = V7X_VMEM_BYTES - 6 * 1024 * 1024
MOD_ROWS = V7X_SUBLANES


def _vmem_limit(buffer_bytes):
    return int(min(VMEM_LIMIT_CAP, buffer_bytes + buffer_bytes // 4 + (8 << 20)))


def _params(semantics, buffer_bytes):
    return pltpu.CompilerParams(dimension_semantics=semantics, vmem_limit_bytes=_vmem_limit(buffer_bytes))


class Layout(NamedTuple):
    bc: int
    lc: int
    bl: int
    ll: int

    @property
    def mc(self):
        return self.bc * self.lc

    @property
    def m(self):
        return self.bc * self.lc + self.bl * self.ll

    def check_tile(self, tm):
        assert self.mc % tm == 0 and self.ll % tm == 0, (self, tm)

    def group(self, i, tm):
        row = i * tm
        return jnp.where(row < self.mc, 0, 1 + (row - self.mc) // self.ll)

    def n_chunks(self, q):
        assert self.lc % q == 0 and self.ll % q == 0
        return self.mc // q + self.bl * (self.ll // q)

    def chunk(self, t, q):
        ncc, ncl, nctx = self.lc // q, self.ll // q, self.mc // q
        is_lat = t >= nctx
        u = jnp.maximum(t - nctx, 0)
        seq = jnp.where(is_lat, u // ncl, t // ncc)
        c = jnp.where(is_lat, lax.rem(u, ncl), lax.rem(t, ncc))
        n = jnp.where(is_lat, ncl, ncc)
        return is_lat, seq, c, n

    def mirror(self, t, q):
        _, _, c, n = self.chunk(t, q)
        return t + n - 1 - 2 * c


def _silu(x):
    return x * jax.nn.sigmoid(x)


def _rmsnorm(x, g):
    ms = jnp.mean(x * x, axis=-1, keepdims=True)
    return x * lax.rsqrt(ms + EPS) * g


MOD_TN = 512


def _mod_kernel(cond_ref, w_ref, b_ref, o_ref):
    s = _silu(cond_ref[...]).astype(BF16)
    w = w_ref[...].astype(BF16)
    o_ref[...] = jnp.dot(s, w, preferred_element_type=F32) + b_ref[...]


def modulation_all(cond, mod_w, mod_b):
    depth, d, n = mod_w.shape
    assert n % MOD_TN == 0
    window = 2 * (d * MOD_TN * 4) + MOD_ROWS * d * 4 + 4 * MOD_ROWS * MOD_TN * 4
    return pl.pallas_call(
        _mod_kernel,
        grid=(depth, n // MOD_TN),
        in_specs=[
            pl.BlockSpec((MOD_ROWS, d), lambda l, j: (0, 0)),
            pl.BlockSpec((None, d, MOD_TN), lambda l, j: (l, 0, j)),
            pl.BlockSpec((None, 1, MOD_TN), lambda l, j: (l, 0, j)),
        ],
        out_specs=pl.BlockSpec((None, MOD_ROWS, MOD_TN), lambda l, j: (l, 0, j)),
        out_shape=jax.ShapeDtypeStruct((depth, MOD_ROWS, n), F32),
        compiler_params=_params(("arbitrary", "arbitrary"), window),
        name="modulation",
    )(cond, mod_w, mod_b.reshape(depth, 1, n))


ADA_TM = 256


def _ada_kernel(*refs, has_y, has_h, res_w):
    refs = list(refs)
    x_ref = refs.pop(0)
    if has_y:
        y_ref, gate_ref, gout_ref = refs.pop(0), refs.pop(0), refs.pop(0)
    if has_h:
        gin_ref, scale_ref, shift_ref = refs.pop(0), refs.pop(0), refs.pop(0)
    x = x_ref[...]
    if has_y:
        xo_ref = refs.pop(0)
        x = x + (res_w * gate_ref[...]) * _rmsnorm(y_ref[...], gout_ref[...])
        xo_ref[...] = x
    if has_h:
        h_ref = refs.pop(0)
        h = _rmsnorm(x, gin_ref[...]) * (1.0 + scale_ref[...]) + shift_ref[...]
        h_ref[...] = h.astype(h_ref.dtype)


def ada_step(lay, x, mod_rows, norm_rows, *, out=None, inn=None):
    m, d = x.shape
    tm = ADA_TM
    lay.check_tile(tm)
    has_y, has_h = out is not None, inn is not None
    res_w = out[2] if has_y else 1.0

    def mod_spec(layer, k):
        return pl.BlockSpec((None, 1, d), lambda i: ((layer * MOD_ROWS + lay.group(i, tm)) * N_MOD + k, 0, 0))

    def norm_spec(layer, k):
        return pl.BlockSpec((None, 1, d), lambda i: (layer * 6 + k, 0, 0))

    row_spec = pl.BlockSpec((tm, d), lambda i: (i, 0))
    args, in_specs, out_shape, out_specs = [x], [row_spec], [], []
    window = tm * d * 4
    if has_y:
        layer, j, _, y = out
        args += [y, mod_rows, norm_rows]
        in_specs += [row_spec, mod_spec(layer, 3 * j + 2), norm_spec(layer, 2 * j + 1)]
        out_shape.append(jax.ShapeDtypeStruct((m, d), F32))
        out_specs.append(row_spec)
        window += 2 * tm * d * 4
    if has_h:
        layer, j = inn
        args += [norm_rows, mod_rows, mod_rows]
        in_specs += [norm_spec(layer, 2 * j), mod_spec(layer, 3 * j + 1), mod_spec(layer, 3 * j)]
        out_shape.append(jax.ShapeDtypeStruct((m, d), BF16))
        out_specs.append(row_spec)
        window += tm * d * 2
    outs = pl.pallas_call(
        functools.partial(_ada_kernel, has_y=has_y, has_h=has_h, res_w=res_w),
        grid=(m // tm,),
        in_specs=in_specs,
        out_specs=out_specs,
        out_shape=out_shape,
        compiler_params=_params(("arbitrary",), 2 * window + 4 * tm * d * 4),
        name="ada_step",
    )(*args)
    outs = list(outs)
    x_new = outs.pop(0) if has_y else None
    h = outs.pop(0) if has_h else None
    return x_new, h


def _mm_kernel(a_ref, w_ref, o_ref):
    o_ref[...] = jnp.dot(a_ref[...], w_ref[...].astype(BF16), preferred_element_type=F32).astype(o_ref.dtype)


def matmul(a, w, lead=(), *, tm, tn, out_dtype=F32):
    m, k = a.shape
    kw, n = w.shape[-2:]
    assert k == kw and m % tm == 0 and n % tn == 0, (a.shape, w.shape, tm, tn)
    nlead = len(lead)
    out_bytes = jnp.dtype(out_dtype).itemsize
    window = tm * k * 2 + 2 * k * tn * 4 + 2 * tm * tn * out_bytes + k * tn * 2 + tm * tn * 4
    return pl.pallas_call(
        _mm_kernel,
        grid=(m // tm, n // tn),
        in_specs=[
            pl.BlockSpec((tm, k), lambda i, j: (i, 0), pipeline_mode=pl.Buffered(1)),
            pl.BlockSpec((None,) * nlead + (k, tn), lambda i, j: tuple(lead) + (0, j)),
        ],
        out_specs=pl.BlockSpec((tm, tn), lambda i, j: (i, j)),
        out_shape=jax.ShapeDtypeStruct((m, n), out_dtype),
        compiler_params=_params(("arbitrary", "arbitrary"), window),
        name="matmul",
    )(a, w)


def _ffn_up_kernel(h_ref, wg_ref, wu_ref, o_ref):
    h = h_ref[...]
    g = jnp.dot(h, wg_ref[...].astype(BF16), preferred_element_type=F32)
    u = jnp.dot(h, wu_ref[...].astype(BF16), preferred_element_type=F32)
    o_ref[...] = (_silu(g) * u).astype(o_ref.dtype)


def ffn_up(h, w_gu, lead, *, tm, tn):
    m, k = h.shape
    n2 = w_gu.shape[-1]
    n = n2 // 2
    assert m % tm == 0 and n % tn == 0
    nj = n // tn
    nlead = len(lead)
    window = tm * k * 2 + 2 * 2 * k * tn * 4 + 2 * tm * tn * 2 + 2 * k * tn * 2 + 3 * tm * tn * 4
    wblock = (None,) * nlead + (k, tn)
    return pl.pallas_call(
        _ffn_up_kernel,
        grid=(m // tm, nj),
        in_specs=[
            pl.BlockSpec((tm, k), lambda i, j: (i, 0), pipeline_mode=pl.Buffered(1)),
            pl.BlockSpec(wblock, lambda i, j: tuple(lead) + (0, j)),
            pl.BlockSpec(wblock, lambda i, j: tuple(lead) + (0, nj + j)),
        ],
        out_specs=pl.BlockSpec((tm, tn), lambda i, j: (i, j)),
        out_shape=jax.ShapeDtypeStruct((m, n), BF16),
        compiler_params=_params(("arbitrary", "arbitrary"), window),
        name="ffn_up",
    )(h, w_gu, w_gu)


ATTN_BQ = 256


def _rope(x, c, s):
    lane = lax.broadcasted_iota(jnp.int32, x.shape, 1)
    half = AXIS_DIM // 2
    partner = jnp.where(lane % AXIS_DIM < half, pltpu.roll(x, HEAD_DIM - half, 1), pltpu.roll(x, half, 1))
    return x * c + partner * s


def _softmax_pv(s, sink, v):
    m = jnp.maximum(jnp.max(s, axis=-1, keepdims=True), sink)
    e = jnp.exp(s - m)
    denom = jnp.sum(e, axis=-1, keepdims=True) + jnp.exp(sink - m)
    p = (e / denom).astype(BF16)
    return jnp.dot(p, v, preferred_element_type=F32)


def _qk(q, k):
    return lax.dot_general(q, k, (((1,), (1,)), ((), ())), preferred_element_type=F32)


def _attn_kernel(sink_ref, q_ref, kc_ref, vc_ref, kp_ref, vp_ref, kn_ref, vn_ref, kx_ref, vx_ref,
                 cq_ref, sq_ref, cp_ref, sp_ref, cn_ref, sn_ref, o_ref, *, lay):
    i = pl.program_id(0)
    kvh = pl.program_id(1)
    bq, w, hd = ATTN_BQ, ATTN_WINDOW, HEAD_DIM
    past = kx_ref.shape[0]

    @pl.when(i < lay.mc // bq)
    def _context():
        k = kc_ref[...].astype(BF16)
        v = vc_ref[...].astype(BF16)
        for g in range(ATTN_GROUP):
            q = q_ref[:, g * hd:(g + 1) * hd].astype(BF16)
            s = _qk(q, k) * ATTN_SCALE
            o_ref[:, g * hd:(g + 1) * hd] = _softmax_pv(s, sink_ref[kvh * ATTN_GROUP + g], v).astype(o_ref.dtype)

    @pl.when(i >= lay.mc // bq)
    def _latent():
        pos = lax.rem(jnp.maximum(i * bq - lay.mc, 0), lay.ll)
        far = 4 * bq
        no_prev = jnp.where(pos > 0, 0, far)
        no_next = jnp.where(pos + bq < lay.ll, 0, far)
        r_c = lax.broadcasted_iota(jnp.int32, (bq, bq), 0)
        c_c = lax.broadcasted_iota(jnp.int32, (bq, bq), 1)
        r_h = lax.broadcasted_iota(jnp.int32, (bq, w), 0)
        c_h = lax.broadcasted_iota(jnp.int32, (bq, w), 1)
        valid = [jnp.abs(r_c - c_c) <= w,
                 c_h >= r_h + no_prev,
                 c_h + no_next <= r_h - (bq - w)]
        bias = jnp.concatenate([jnp.where(t, 0.0, NEG_INF).astype(F32) for t in valid]
                               + [jnp.zeros((bq, past), F32)], axis=1)
        cq, sq = cq_ref[...], sq_ref[...]
        k = jnp.concatenate([_rope(kc_ref[...], cq, sq), _rope(kp_ref[...], cp_ref[...], sp_ref[...]),
                             _rope(kn_ref[...], cn_ref[...], sn_ref[...]), kx_ref[...]], axis=0).astype(BF16)
        v = jnp.concatenate([vc_ref[...], vp_ref[...], vn_ref[...], vx_ref[...]], axis=0).astype(BF16)
        for g in range(ATTN_GROUP):
            q = _rope(q_ref[:, g * hd:(g + 1) * hd], cq, sq).astype(BF16)
            s = _qk(q, k) * ATTN_SCALE + bias
            o_ref[:, g * hd:(g + 1) * hd] = _softmax_pv(s, sink_ref[kvh * ATTN_GROUP + g], v).astype(o_ref.dtype)


def rope_tables(n_tokens):
    rows = n_tokens // GRID_W
    r = jnp.repeat(jnp.arange(rows), GRID_W).astype(F32)
    col = jnp.tile(jnp.arange(GRID_W), rows).astype(F32)
    inv = ROPE_THETA ** (-jnp.arange(0, AXIS_DIM, 2, dtype=F32) / AXIS_DIM)
    ang = jnp.stack([r[:, None] * inv, col[:, None] * inv], axis=1)
    cos, sin = jnp.cos(ang), jnp.sin(ang)
    c = jnp.stack([cos, cos], axis=2).reshape(n_tokens, HEAD_DIM)
    s = jnp.stack([-sin, sin], axis=2).reshape(n_tokens, HEAD_DIM)
    return c, s


def attention(lay, qkv, sink, cache_k, cache_v, slot, rope_c, rope_s):
    m = qkv.shape[0]
    bq, w, hd = ATTN_BQ, ATTN_WINDOW, HEAD_DIM
    assert lay.lc == bq and lay.ll % bq == 0 and bq % w == 0
    past = cache_k.shape[2]
    qw = ATTN_GROUP * hd
    kcol, vcol = ATTN_NQ // hd, (ATTN_NQ + ATTN_NK) // hd
    per = bq // w

    def pos(i):
        return lax.rem(jnp.maximum(i * bq - lay.mc, 0), lay.ll)

    def prev_row(i):
        return jnp.maximum(i * per - 1, 0)

    def next_row(i):
        return jnp.minimum((i + 1) * per, m // w - 1)

    def cache_map(i, h):
        return (jnp.maximum(i * bq - lay.mc, 0) // lay.ll, slot, 0, h)

    tab_q = pl.BlockSpec((bq, hd), lambda i, h: (pos(i) // bq, 0))
    tab_p = pl.BlockSpec((w, hd), lambda i, h: (jnp.maximum(pos(i) - w, 0) // w, 0))
    tab_n = pl.BlockSpec((w, hd), lambda i, h: (jnp.minimum(pos(i) + bq, lay.ll - w) // w, 0))
    in_specs = [
        pl.BlockSpec(memory_space=pltpu.SMEM),
        pl.BlockSpec((bq, qw), lambda i, h: (i, h)),
        pl.BlockSpec((bq, hd), lambda i, h: (i, kcol + h)),
        pl.BlockSpec((bq, hd), lambda i, h: (i, vcol + h)),
        pl.BlockSpec((w, hd), lambda i, h: (prev_row(i), kcol + h)),
        pl.BlockSpec((w, hd), lambda i, h: (prev_row(i), vcol + h)),
        pl.BlockSpec((w, hd), lambda i, h: (next_row(i), kcol + h)),
        pl.BlockSpec((w, hd), lambda i, h: (next_row(i), vcol + h)),
        pl.BlockSpec((None, None, past, hd), cache_map),
        pl.BlockSpec((None, None, past, hd), cache_map),
        tab_q, tab_q, tab_p, tab_p, tab_n, tab_n,
    ]
    nk = bq + 2 * w + past
    window = 2 * 4 * (bq * qw + 2 * (bq + 2 * w + past) * hd + 6 * bq * hd) + 2 * bq * qw * 2
    work = 4 * bq * nk * 4 + 4 * nk * hd * 2
    return pl.pallas_call(
        functools.partial(_attn_kernel, lay=lay),
        grid=(m // bq, ATTN_KV_HEADS),
        in_specs=in_specs,
        out_specs=pl.BlockSpec((bq, qw), lambda i, h: (i, h)),
        out_shape=jax.ShapeDtypeStruct((m, ATTN_NQ), BF16),
        compiler_params=_params(("arbitrary", "arbitrary"), window + work),
        name="attention",
    )(sink, qkv, qkv, qkv, qkv, qkv, qkv, qkv, cache_k, cache_v, rope_c, rope_s, rope_c, rope_s, rope_c, rope_s)


def _split3(x):
    hi = x.astype(BF16)
    r = x - hi.astype(F32)
    mid = r.astype(BF16)
    lo = (r - mid.astype(F32)).astype(BF16)
    return hi, mid, lo


def _select_rows(sel, x):
    return sum(jnp.dot(sel, p, preferred_element_type=F32) for p in _split3(x))


def _select_rows_t(sel, x):
    return sum(_qk(sel, p) for p in _split3(x))


def _tri(q, fwd):
    r = lax.broadcasted_iota(jnp.int32, (q, q), 0)
    c = lax.broadcasted_iota(jnp.int32, (q, q), 1)
    return (r >= c) if fwd else (r <= c)


def _eye(n, k):
    return (lax.broadcasted_iota(jnp.int32, (n, k), 0) == lax.broadcasted_iota(jnp.int32, (n, k), 1)).astype(BF16)


CONV_TR = 256
CONV_TC = 512
CONV_HALO = V7X_SUBLANES


def _ssd_conv_kernel(x_ref, xp_ref, xn_ref, w_ref, b_ref, o_ref, *, lay):
    tr = x_ref.shape[0]
    row0 = pl.program_id(0) * tr
    in_lat = row0 >= lay.mc
    pos = jnp.where(in_lat, lax.rem(jnp.maximum(row0 - lay.mc, 0), lay.ll), lax.rem(row0, lay.lc))
    seq_len = jnp.where(in_lat, lay.ll, lay.lc)
    xp = jnp.where(pos > 0, xp_ref[...], 0.0)
    xn = jnp.where(pos + tr < seq_len, xn_ref[...], 0.0)
    ext = jnp.concatenate([xp, x_ref[...], xn], axis=0)
    n = tr + 2 * CONV_HALO
    acc = jnp.zeros(x_ref.shape, F32) + b_ref[...]
    for k in range(SSD_CONV):
        d = k - SSD_CONV // 2
        tap = pltpu.roll(ext, (n - d) % n, 0)[CONV_HALO:CONV_HALO + tr]
        acc = acc + tap * w_ref[k:k + 1, :]
    o_ref[...] = _silu(acc)


def ssd_conv(lay, proj, conv_w, conv_b, slot):
    m = proj.shape[0]
    tr, tc, halo = CONV_TR, CONV_TC, CONV_HALO
    lay.check_tile(tr)
    assert lay.lc % tr == 0 and SSD_INNER % tc == 0 and SSD_CONV_CH % tc == 0
    col0 = SSD_INNER // tc
    per = tr // halo
    n_slots, _, ch = conv_w.shape
    return pl.pallas_call(
        functools.partial(_ssd_conv_kernel, lay=lay),
        grid=(m // tr, ch // tc),
        in_specs=[
            pl.BlockSpec((tr, tc), lambda i, j: (i, col0 + j)),
            pl.BlockSpec((halo, tc), lambda i, j: (jnp.maximum(i * per - 1, 0), col0 + j)),
            pl.BlockSpec((halo, tc), lambda i, j: (jnp.minimum((i + 1) * per, m // halo - 1), col0 + j)),
            pl.BlockSpec((None, SSD_CONV, tc), lambda i, j: (slot, 0, j)),
            pl.BlockSpec((None, 1, tc), lambda i, j: (slot, 0, j)),
        ],
        out_specs=pl.BlockSpec((tr, tc), lambda i, j: (i, j)),
        out_shape=jax.ShapeDtypeStruct((m, ch), F32),
        compiler_params=_params(("arbitrary", "arbitrary"), 12 * tr * tc * 4),
        name="ssd_conv",
    )(proj, proj, proj, conv_w, conv_b.reshape(n_slots, 1, ch))


SSD_PAIR = 2 * SSD_HEAD_DIM
SSD_GROUP_W = SSD_GROUP_HEADS * SSD_HEAD_DIM


def _ssd_scan_kernel(*refs, lay, has_h0):
    refs = list(refs)
    dirs = [tuple(refs.pop(0) for _ in range(4)) for _ in range(2)]
    bias_ref, alog_ref = refs.pop(0), refs.pop(0)
    h0_ref = refs.pop(0) if has_h0 else None
    y_refs = (refs.pop(0), refs.pop(0))
    st_ref, ht_ref = refs.pop(0), refs.pop(0)
    g = pl.program_id(0)
    q, n_pairs = SSD_CHUNK, SSD_GROUP_W // SSD_PAIR
    is_lat, _, c, nc = lay.chunk(pl.program_id(1), q)

    @pl.when(c == 0)
    def _init():
        for d in range(2):
            for pr in range(n_pairs):
                blk = slice(pr * SSD_PAIR, (pr + 1) * SSD_PAIR)
                if has_h0:
                    ht_ref[d, :, blk] = jnp.where(is_lat, h0_ref[d, blk, :].T, 0.0)
                else:
                    ht_ref[d, :, blk] = jnp.zeros((SSD_STATE, SSD_PAIR), F32)

    lane = lax.broadcasted_iota(jnp.int32, (q, SSD_PAIR), 1)
    low = lane < SSD_HEAD_DIM
    low_row = low[0:1, :]
    group_shift = lax.rem(V7X_LANES - g * SSD_GROUP_HEADS, V7X_LANES)
    for d, (x_ref, b_ref, c_ref, dt_ref) in enumerate(dirs):
        fwd = d == 0
        tri = _tri(q, fwd)
        dt = jax.nn.softplus(dt_ref[...] + bias_ref[d:d + 1, :])
        a = dt * (-jnp.exp(alog_ref[d:d + 1, :]))
        acum = pltpu.roll(_select_rows(tri.astype(BF16), a), group_shift, 1)
        dt = pltpu.roll(dt, group_shift, 1)
        pick = _eye(SSD_GROUP_HEADS, V7X_LANES)
        acum_t = _select_rows_t(pick, acum)
        dt_t = _select_rows_t(pick, dt)
        total = acum[q - 1:q, :] if fwd else acum[0:1, :]
        e_acum = jnp.exp(acum)
        to_end = jnp.exp(total - acum) * dt
        e_total = jnp.exp(total)
        bm = b_ref[...].astype(BF16)
        cm = c_ref[...].astype(BF16)
        cb = _qk(cm, bm)
        bm_t = _qk(_eye(SSD_STATE, SSD_STATE), bm).astype(BF16)
        for pr in range(n_pairs):
            blk = slice(pr * SSD_PAIR, (pr + 1) * SSD_PAIR)
            r0, r1 = 2 * pr, 2 * pr + 1

            def pair(v, low_mask):
                return jnp.where(low_mask, v[:, r0:r0 + 1], v[:, r1:r1 + 1])

            xp = x_ref[:, blk]
            y = jnp.zeros((q, SSD_PAIR), F32)
            for r, mask in ((r0, low), (r1, jnp.logical_not(low))):
                seg = acum[:, r:r + 1] - acum_t[r:r + 1, :]
                decay = jnp.exp(jnp.where(tri, seg, -jnp.inf))
                w = (cb * decay * dt_t[r:r + 1, :]).astype(BF16)
                y = y + jnp.dot(w, jnp.where(mask, xp, 0.0).astype(BF16), preferred_element_type=F32)
            ht = ht_ref[d, :, blk]
            y = y + jnp.dot(cm, ht.astype(BF16), preferred_element_type=F32) * pair(e_acum, low)
            y_refs[d][:, blk] = y
            xs = (xp * pair(to_end, low)).astype(BF16)
            ht_ref[d, :, blk] = ht * pair(e_total, low_row) + jnp.dot(bm_t, xs, preferred_element_type=F32)

    @pl.when(jnp.logical_and(c == nc - 1, jnp.logical_not(is_lat)))
    def _emit():
        for d in range(2):
            for pr in range(n_pairs):
                blk = slice(pr * SSD_PAIR, (pr + 1) * SSD_PAIR)
                st_ref[d, blk, :] = ht_ref[d, :, blk].T


def ssd_scan(lay, proj, xbc, dt_bias, a_log, slot, h0):
    m = proj.shape[0]
    q = SSD_CHUNK
    gw, st = SSD_GROUP_W, SSD_STATE
    b_col, c_col = SSD_INNER // st, (SSD_INNER + SSD_GN) // st
    dt_col = (SSD_INNER + SSD_CONV_CH) // V7X_LANES
    assert SSD_HEADS == V7X_LANES

    def row_f(g, t):
        return t

    def row_b(g, t):
        return lay.mirror(t, q)

    def dir_specs(row, d):
        return [
            pl.BlockSpec((q, gw), lambda g, t: (row(g, t), g)),
            pl.BlockSpec((q, st), lambda g, t: (row(g, t), b_col + g)),
            pl.BlockSpec((q, st), lambda g, t: (row(g, t), c_col + g)),
            pl.BlockSpec((q, V7X_LANES), lambda g, t: (row(g, t), dt_col + d)),
        ]

    def seq_ctx(t):
        is_lat, seq, _, _ = lay.chunk(t, q)
        return jnp.where(is_lat, lay.bc - 1, seq)

    def seq_lat(t):
        is_lat, seq, _, _ = lay.chunk(t, q)
        return jnp.where(is_lat, seq, 0)

    par_spec = pl.BlockSpec((None, 2, SSD_HEADS), lambda g, t: (slot, 0, 0))
    in_specs = dir_specs(row_f, 0) + dir_specs(row_b, 1) + [par_spec, par_spec]
    args = [xbc, xbc, xbc, proj, xbc, xbc, xbc, proj, dt_bias, a_log]
    has_h0 = h0 is not None
    if has_h0:
        in_specs.append(pl.BlockSpec((None, None, 2, gw, st), lambda g, t: (seq_lat(t), slot, 0, g, 0)))
        args.append(h0)
    y_shape = jax.ShapeDtypeStruct((m, SSD_INNER), F32)
    window = 2 * 4 * (2 * 2 * q * gw + 6 * q * st + 2 * 2 * gw * st) + 2 * st * gw * 4
    return pl.pallas_call(
        functools.partial(_ssd_scan_kernel, lay=lay, has_h0=has_h0),
        grid=(SSD_GROUPS, lay.n_chunks(q)),
        in_specs=in_specs,
        out_specs=[
            pl.BlockSpec((q, gw), lambda g, t: (row_f(g, t), g)),
            pl.BlockSpec((q, gw), lambda g, t: (row_b(g, t), g)),
            pl.BlockSpec((None, 2, gw, st), lambda g, t: (seq_ctx(t), 0, g, 0)),
        ],
        out_shape=[y_shape, y_shape, jax.ShapeDtypeStruct((lay.bc, 2, SSD_INNER, st), F32)],
        scratch_shapes=[pltpu.VMEM((2, st, gw), F32)],
        compiler_params=_params(("arbitrary", "arbitrary"), window),
        name="ssd_scan",
    )(*args)


SSD_GATE_TM = 256


def _ssd_gate_kernel(yf_ref, yb_ref, x_ref, z_ref, d_ref, g_ref, o_ref):
    y = (yf_ref[...] + yb_ref[...] + d_ref[...] * x_ref[...]) * _silu(z_ref[...])
    o_ref[...] = _rmsnorm(y, g_ref[...]).astype(o_ref.dtype)


def ssd_gate(lay, y_f, y_b, xbc, proj, d_skip, norm_g, slot):
    m = proj.shape[0]
    tm, gw = SSD_GATE_TM, SSD_INNER // SSD_GROUPS
    n_slots = norm_g.shape[0]
    d_cols = jnp.repeat(d_skip[slot], SSD_HEAD_DIM).reshape(1, SSD_INNER)
    blk = pl.BlockSpec((tm, gw), lambda i, g: (i, g))
    return pl.pallas_call(
        _ssd_gate_kernel,
        grid=(m // tm, SSD_GROUPS),
        in_specs=[blk, blk, blk, blk,
                  pl.BlockSpec((1, gw), lambda i, g: (0, g)),
                  pl.BlockSpec((None, 1, gw), lambda i, g: (slot, 0, g))],
        out_specs=blk,
        out_shape=jax.ShapeDtypeStruct((m, SSD_INNER), BF16),
        compiler_params=_params(("arbitrary", "arbitrary"), 16 * tm * gw * 4),
        name="ssd_gate",
    )(y_f, y_b, xbc, proj, d_cols, norm_g.reshape(n_slots, 1, SSD_INNER))


HGRN_HB = 8
HGRN_W = HGRN_HB * HGRN_DK


def _hgrn_scan_kernel(*refs, lay, layer, has_s0):
    refs = list(refs)
    dirs = [tuple(refs.pop(0) for _ in range(3)) for _ in range(2)]
    lb_ref = refs.pop(0)
    s0_ref = refs.pop(0) if has_s0 else None
    y_refs = (refs.pop(0), refs.pop(0))
    st_ref, s_ref = refs.pop(0), refs.pop(0)
    q = HGRN_CHUNK
    is_lat, _, c, nc = lay.chunk(pl.program_id(1), q)

    @pl.when(c == 0)
    def _init():
        for d in range(2):
            for hh in range(HGRN_HB):
                if has_s0:
                    s_ref[d, hh] = jnp.where(is_lat, s0_ref[d, hh].T, 0.0)
                else:
                    s_ref[d, hh] = jnp.zeros((HGRN_DV, HGRN_DK), F32)

    logits = lb_ref[...]
    e = jnp.exp(logits - jnp.max(logits, axis=0, keepdims=True))
    p = e / jnp.sum(e, axis=0, keepdims=True)
    lb = jnp.zeros(p.shape[1:], F32)
    for l in range(1, layer + 1):
        lb = lb + p[l]

    eye = _eye(HGRN_DV, HGRN_DV)
    for d, (q_ref, v_ref, f_ref) in enumerate(dirs):
        fwd = d == 0
        tri = _tri(q, fwd)
        lbd = lb[d:d + 1, :]
        fr = f_ref[...]
        f = lbd + (1.0 - lbd) * jax.nn.sigmoid(fr)
        k = (1.0 - lbd) * jax.nn.sigmoid(-fr)
        b = _select_rows(tri.astype(BF16), jnp.log(f))
        total = b[q - 1:q, :] if fwd else b[0:1, :]
        qd = (_silu(q_ref[...]) * jnp.exp(b)).astype(BF16)
        kd = (k * jnp.exp(-b)).astype(BF16)
        ks = (k * jnp.exp(total - b)).astype(BF16)
        dec = jnp.exp(total)
        v = v_ref[...].astype(BF16)
        for hh in range(HGRN_HB):
            sl = slice(hh * HGRN_DK, (hh + 1) * HGRN_DK)
            att = jnp.where(tri, _qk(qd[:, sl], kd[:, sl]), 0.0).astype(BF16)
            s_prev = s_ref[d, hh]
            o = jnp.dot(att, v[:, sl], preferred_element_type=F32) + _qk(qd[:, sl], s_prev.astype(BF16))
            y_refs[d][:, sl] = o
            v_t = _qk(eye, v[:, sl]).astype(BF16)
            s_ref[d, hh] = s_prev * dec[:, sl] + jnp.dot(v_t, ks[:, sl], preferred_element_type=F32)

    @pl.when(jnp.logical_and(c == nc - 1, jnp.logical_not(is_lat)))
    def _emit():
        for d in range(2):
            for hh in range(HGRN_HB):
                st_ref[d, hh] = s_ref[d, hh].T


def hgrn_scan(lay, proj, lb_logits, layer, slot, s0):
    m = proj.shape[0]
    q, w = HGRN_CHUNK, HGRN_W
    v_col, f_col = HGRN_HK // w, (HGRN_HK + HGRN_HV) // w
    fb_col = f_col + HGRN_HK // w

    def row_f(t):
        return t

    def row_b(t):
        return lay.mirror(t, q)

    def dir_specs(row, fcol):
        return [
            pl.BlockSpec((q, w), lambda hb, t: (row(t), hb)),
            pl.BlockSpec((q, w), lambda hb, t: (row(t), v_col + hb)),
            pl.BlockSpec((q, w), lambda hb, t: (row(t), fcol + hb)),
        ]

    def seq_ctx(t):
        is_lat, seq, _, _ = lay.chunk(t, q)
        return jnp.where(is_lat, lay.bc - 1, seq)

    def seq_lat(t):
        is_lat, seq, _, _ = lay.chunk(t, q)
        return jnp.where(is_lat, seq, 0)

    in_specs = dir_specs(row_f, f_col) + dir_specs(row_b, fb_col)
    in_specs.append(pl.BlockSpec((DEPTH, 2, w), lambda hb, t: (0, 0, hb)))
    args = [proj] * 6 + [lb_logits]
    has_s0 = s0 is not None
    if has_s0:
        in_specs.append(pl.BlockSpec((None, None, 2, HGRN_HB, HGRN_DK, HGRN_DV), lambda hb, t: (seq_lat(t), slot, 0, hb, 0, 0)))
        args.append(s0)
    y_shape = jax.ShapeDtypeStruct((m, HGRN_HV), F32)
    state_bytes = 2 * HGRN_HB * HGRN_DK * HGRN_DV * 4
    window = 2 * 4 * (8 * q * w) + 5 * state_bytes + 16 * q * w * 4
    return pl.pallas_call(
        functools.partial(_hgrn_scan_kernel, lay=lay, layer=layer, has_s0=has_s0),
        grid=(HGRN_HEADS // HGRN_HB, lay.n_chunks(q)),
        in_specs=in_specs,
        out_specs=[
            pl.BlockSpec((q, w), lambda hb, t: (row_f(t), hb)),
            pl.BlockSpec((q, w), lambda hb, t: (row_b(t), hb)),
            pl.BlockSpec((None, 2, HGRN_HB, HGRN_DK, HGRN_DV), lambda hb, t: (seq_ctx(t), 0, hb, 0, 0)),
        ],
        out_shape=[y_shape, y_shape, jax.ShapeDtypeStruct((lay.bc, 2, HGRN_HEADS, HGRN_DK, HGRN_DV), F32)],
        scratch_shapes=[pltpu.VMEM((2, HGRN_HB, HGRN_DV, HGRN_DK), F32)],
        compiler_params=_params(("arbitrary", "arbitrary"), window),
        name="hgrn_scan",
    )(*args)


HGRN_NORM_TM = 256


def _hgrn_norm_kernel(yf_ref, yb_ref, g_ref, ng_ref, o_ref):
    ng = ng_ref[...]
    for hh in range(HGRN_HEADS):
        sl = slice(hh * HGRN_DV, (hh + 1) * HGRN_DV)
        o = yf_ref[:, sl] + yb_ref[:, sl]
        o_ref[:, sl] = (_rmsnorm(o, ng) * _silu(g_ref[:, sl])).astype(o_ref.dtype)


def hgrn_norm(lay, y_f, y_b, proj, norm_g, slot):
    m = proj.shape[0]
    tm = HGRN_NORM_TM
    n_slots = norm_g.shape[0]
    g_col = (3 * HGRN_HK + HGRN_HV) // HGRN_HV
    blk = pl.BlockSpec((tm, HGRN_HV), lambda i: (i, 0))
    return pl.pallas_call(
        _hgrn_norm_kernel,
        grid=(m // tm,),
        in_specs=[blk, blk, pl.BlockSpec((tm, HGRN_HV), lambda i: (i, g_col)),
                  pl.BlockSpec((None, 1, HGRN_DV), lambda i: (slot, 0, 0))],
        out_specs=blk,
        out_shape=jax.ShapeDtypeStruct((m, HGRN_HV), BF16),
        compiler_params=_params(("arbitrary",), 10 * tm * HGRN_HV * 4),
        name="hgrn_norm",
    )(y_f, y_b, proj, norm_g.reshape(n_slots, 1, HGRN_DV))


MM_TM = 1024
FFN_TN = 256
PROJ_TN = 512
SSD_IN_TN = 256


def kernel(x_prompt, x_sample, cache_attn_k, cache_attn_v, state_ssd, state_hgrn, c, c_ctx, mod_w, mod_b, norm_g, ffn_w_gu, ffn_w_down, attn_w_in, attn_sink, attn_w_out, ssd_w_in, ssd_conv_w, ssd_conv_b, ssd_dt_bias, ssd_A_log, ssd_D, ssd_norm_g, ssd_w_out, hgrn_w_in, hgrn_lb_logits, hgrn_norm_g, hgrn_w_out):
    bc, lc, d = x_prompt.shape
    bl, ll, _ = x_sample.shape
    lay = Layout(bc, lc, bl, ll)
    lay.check_tile(MM_TM)
    assert d == D_MODEL and 1 + bl <= MOD_ROWS
    mc = lay.mc
    x = jnp.concatenate([x_prompt.reshape(mc, d), x_sample.reshape(bl * ll, d)], axis=0)

    cond = jnp.concatenate([c_ctx[None, :], c, jnp.zeros((MOD_ROWS - 1 - bl, d), F32)], axis=0)
    mod_rows = modulation_all(cond, mod_w, mod_b).reshape(DEPTH * MOD_ROWS * N_MOD, 1, d)
    norm_rows = norm_g.reshape(DEPTH * 6, 1, d)
    rope_c, rope_s = rope_tables(ll)
    past = cache_attn_k.shape[2]
    cache_k = cache_attn_k.reshape(bl, -1, past, ATTN_NK)
    cache_v = cache_attn_v.reshape(bl, -1, past, ATTN_NK)
    ssd_h0 = state_ssd.reshape(bl, -1, 2, SSD_INNER, SSD_STATE)

    def ffn(h, layer, half):
        a = ffn_up(h, ffn_w_gu, (layer, half), tm=MM_TM, tn=FFN_TN)
        return matmul(a, ffn_w_down, (layer, half), tm=MM_TM, tn=FFN_TN)

    new_k, new_v, new_ssd, new_hgrn = [], [], [], []
    _, h = ada_step(lay, x, mod_rows, norm_rows, inn=(0, 0))
    for layer in range(DEPTH):
        x, h = ada_step(lay, x, mod_rows, norm_rows, out=(layer, 0, FFN_RES_W, ffn(h, layer, 0)), inn=(layer, 1))
        kind, slot = layer % N_MIXERS, layer // N_MIXERS
        if kind == 0:
            qkv = matmul(h, attn_w_in, (slot,), tm=MM_TM, tn=PROJ_TN)
            new_k.append(qkv[:mc, ATTN_NQ:ATTN_NQ + ATTN_NK].reshape(bc, lc, ATTN_KV_HEADS, HEAD_DIM))
            new_v.append(qkv[:mc, ATTN_NQ + ATTN_NK:].reshape(bc, lc, ATTN_KV_HEADS, HEAD_DIM))
            o = attention(lay, qkv, attn_sink[slot], cache_k, cache_v, slot, rope_c, rope_s)
            y = matmul(o, attn_w_out, (slot,), tm=MM_TM, tn=PROJ_TN)
        elif kind == 1:
            proj = matmul(h, ssd_w_in, (slot,), tm=MM_TM, tn=SSD_IN_TN)
            xbc = ssd_conv(lay, proj, ssd_conv_w, ssd_conv_b, slot)
            y_f, y_b, st = ssd_scan(lay, proj, xbc, ssd_dt_bias, ssd_A_log, slot, ssd_h0)
            new_ssd.append(st.reshape(bc, 2, SSD_HEADS, SSD_HEAD_DIM, SSD_STATE))
            o = ssd_gate(lay, y_f, y_b, xbc, proj, ssd_D, ssd_norm_g, slot)
            y = matmul(o, ssd_w_out, (slot,), tm=MM_TM, tn=FFN_TN)
        else:
            proj = matmul(h, hgrn_w_in, (slot,), tm=MM_TM, tn=PROJ_TN)
            y_f, y_b, st = hgrn_scan(lay, proj, hgrn_lb_logits, layer, slot, state_hgrn)
            new_hgrn.append(st)
            o = hgrn_norm(lay, y_f, y_b, proj, hgrn_norm_g, slot)
            y = matmul(o, hgrn_w_out, (slot,), tm=MM_TM, tn=PROJ_TN)
        x, h = ada_step(lay, x, mod_rows, norm_rows, out=(layer, 1, 1.0, y), inn=(layer, 2))
        nxt = (layer + 1, 0) if layer + 1 < DEPTH else None
        x, h = ada_step(lay, x, mod_rows, norm_rows, out=(layer, 2, FFN_RES_W, ffn(h, layer, 1)), inn=nxt)

    return (x[:mc].reshape(bc, lc, d), x[mc:].reshape(bl, ll, d),
            jnp.stack(new_k, axis=1), jnp.stack(new_v, axis=1),
            jnp.stack(new_ssd, axis=1).astype(state_ssd.dtype), jnp.stack(new_hgrn, axis=1).astype(state_hgrn.dtype))
```

```python
import functools
import math
from typing import NamedTuple

import jax
import jax.numpy as jnp
from jax import lax
from jax.experimental import pallas as pl
from jax.experimental.pallas import tpu as pltpu

F32 = jnp.float32
BF16 = jnp.bfloat16

D_MODEL = 4096
DEPTH = 4
GRID_W = 64
N_MIXERS = 3
ATTN_HEADS = 32
ATTN_KV_HEADS = 8
ATTN_GROUP = ATTN_HEADS // ATTN_KV_HEADS
HEAD_DIM = D_MODEL // ATTN_HEADS
AXIS_DIM = HEAD_DIM // 2
ATTN_WINDOW = 128
ATTN_SCALE = HEAD_DIM ** -0.5
ROPE_THETA = 10000.0
ATTN_NQ = ATTN_HEADS * HEAD_DIM
ATTN_NK = ATTN_KV_HEADS * HEAD_DIM
ATTN_IN = ATTN_NQ + 2 * ATTN_NK
SSD_INNER = 2 * D_MODEL
SSD_HEAD_DIM = 64
SSD_HEADS = SSD_INNER // SSD_HEAD_DIM
SSD_GROUPS = 8
SSD_GROUP_HEADS = SSD_HEADS // SSD_GROUPS
SSD_STATE = 128
SSD_GN = SSD_GROUPS * SSD_STATE
SSD_CONV = 5
SSD_CONV_CH = SSD_INNER + 2 * SSD_GN
SSD_CHUNK = 64
SSD_IN = SSD_INNER + SSD_CONV_CH + 2 * SSD_HEADS
HGRN_DK = 128
HGRN_HEADS = D_MODEL // HGRN_DK
HGRN_DV = D_MODEL // HGRN_HEADS
HGRN_HK = HGRN_HEADS * HGRN_DK
HGRN_HV = HGRN_HEADS * HGRN_DV
HGRN_CHUNK = 64
HGRN_IN = 3 * HGRN_HK + 2 * HGRN_HV
D_FF = 11008
FFN_RES_W = 0.5
N_MOD = 9
EPS = 1e-6
NEG_INF = -1e30

V7X_VMEM_BYTES = 64 * 1024 * 1024
V7X_SUBLANES = 8
V7X_LANES = 128
VMEM_LIMIT_CAP = V7X_VMEM_BYTES - 6 * 1024 * 1024
MOD_ROWS = V7X_SUBLANES


def _vmem_limit(buffer_bytes):
    return int(min(VMEM_LIMIT_CAP, buffer_bytes + buffer_bytes // 4 + (8 << 20)))


def _params(semantics, buffer_bytes):
    return pltpu.CompilerParams(dimension_semantics=semantics, vmem_limit_bytes=_vmem_limit(buffer_bytes))


class Layout(NamedTuple):
    bc: int
    lc: int
    bl: int
    ll: int

    @property
    def mc(self):
        return self.bc * self.lc

    @property
    def m(self):
        return self.bc * self.lc + self.bl * self.ll

    def check_tile(self, tm):
        assert self.mc % tm == 0 and self.ll % tm == 0, (self, tm)

    def group(self, i, tm):
        row = i * tm
        return jnp.where(row < self.mc, 0, 1 + (row - self.mc) // self.ll)

    def n_chunks(self, q):
        assert self.lc % q == 0 and self.ll % q == 0
        return self.mc // q + self.bl * (self.ll // q)

    def chunk(self, t, q):
        ncc, ncl, nctx = self.lc // q, self.ll // q, self.mc // q
        is_lat = t >= nctx
        u = jnp.maximum(t - nctx, 0)
        seq = jnp.where(is_lat, u // ncl, t // ncc)
        c = jnp.where(is_lat, lax.rem(u, ncl), lax.rem(t, ncc))
        n = jnp.where(is_lat, ncl, ncc)
        return is_lat, seq, c, n

    def mirror(self, t, q):
        _, _, c, n = self.chunk(t, q)
        return t + n - 1 - 2 * c


def _silu(x):
    return x * jax.nn.sigmoid(x)


def _rmsnorm(x, g):
    ms = jnp.mean(x * x, axis=-1, keepdims=True)
    return x * lax.rsqrt(ms + EPS) * g


MOD_TN = 512


def _mod_kernel(cond_ref, w_ref, b_ref, o_ref):
    s = _silu(cond_ref[...]).astype(BF16)
    w = w_ref[...].astype(BF16)
    o_ref[...] = jnp.dot(s, w, preferred_element_type=F32) + b_ref[...]


def modulation_all(cond, mod_w, mod_b):
    depth, d, n = mod_w.shape
    assert n % MOD_TN == 0
    window = 2 * (d * MOD_TN * 4) + MOD_ROWS * d * 4 + 4 * MOD_ROWS * MOD_TN * 4
    return pl.pallas_call(
        _mod_kernel,
        grid=(depth, n // MOD_TN),
        in_specs=[
            pl.BlockSpec((MOD_ROWS, d), lambda l, j: (0, 0)),
            pl.BlockSpec((None, d, MOD_TN), lambda l, j: (l, 0, j)),
            pl.BlockSpec((None, 1, MOD_TN), lambda l, j: (l, 0, j)),
        ],
        out_specs=pl.BlockSpec((None, MOD_ROWS, MOD_TN), lambda l, j: (l, 0, j)),
        out_shape=jax.ShapeDtypeStruct((depth, MOD_ROWS, n), F32),
        compiler_params=_params(("arbitrary", "arbitrary"), window),
        name="modulation",
    )(cond, mod_w, mod_b.reshape(depth, 1, n))


ADA_TM = 256


def _ada_kernel(*refs, has_y, has_h, res_w):
    refs = list(refs)
    x_ref = refs.pop(0)
    if has_y:
        y_ref, gate_ref, gout_ref = refs.pop(0), refs.pop(0), refs.pop(0)
    if has_h:
        gin_ref, scale_ref, shift_ref = refs.pop(0), refs.pop(0), refs.pop(0)
    x = x_ref[...]
    if has_y:
        xo_ref = refs.pop(0)
        x = x + (res_w * gate_ref[...]) * _rmsnorm(y_ref[...], gout_ref[...])
        xo_ref[...] = x
    if has_h:
        h_ref = refs.pop(0)
        h = _rmsnorm(x, gin_ref[...]) * (1.0 + scale_ref[...]) + shift_ref[...]
        h_ref[...] = h.astype(h_ref.dtype)


def ada_step(lay, x, mod_rows, norm_rows, *, out=None, inn=None):
    m, d = x.shape
    tm = ADA_TM
    lay.check_tile(tm)
    has_y, has_h = out is not None, inn is not None
    res_w = out[2] if has_y else 1.0

    def mod_spec(layer, k):
        return pl.BlockSpec((None, 1, d), lambda i: ((layer * MOD_ROWS + lay.group(i, tm)) * N_MOD + k, 0, 0))

    def norm_spec(layer, k):
        return pl.BlockSpec((None, 1, d), lambda i: (layer * 6 + k, 0, 0))

    row_spec = pl.BlockSpec((tm, d), lambda i: (i, 0))
    args, in_specs, out_shape, out_specs = [x], [row_spec], [], []
    window = tm * d * 4
    if has_y:
        layer, j, _, y = out
        args += [y, mod_rows, norm_rows]
        in_specs += [row_spec, mod_spec(layer, 3 * j + 2), norm_spec(layer, 2 * j + 1)]
        out_shape.append(jax.ShapeDtypeStruct((m, d), F32))
        out_specs.append(row_spec)
        window += 2 * tm * d * 4
    if has_h:
        layer, j = inn
        args += [norm_rows, mod_rows, mod_rows]
        in_specs += [norm_spec(layer, 2 * j), mod_spec(layer, 3 * j + 1), mod_spec(layer, 3 * j)]
        out_shape.append(jax.ShapeDtypeStruct((m, d), BF16))
        out_specs.append(row_spec)
        window += tm * d * 2
    outs = pl.pallas_call(
        functools.partial(_ada_kernel, has_y=has_y, has_h=has_h, res_w=res_w),
        grid=(m // tm,),
        in_specs=in_specs,
        out_specs=out_specs,
        out_shape=out_shape,
        compiler_params=_params(("arbitrary",), 2 * window + 4 * tm * d * 4),
        name="ada_step",
    )(*args)
    outs = list(outs)
    x_new = outs.pop(0) if has_y else None
    h = outs.pop(0) if has_h else None
    return x_new, h


def _mm_kernel(a_ref, w_ref, o_ref):
    o_ref[...] = jnp.dot(a_ref[...], w_ref[...].astype(BF16), preferred_element_type=F32).astype(o_ref.dtype)


def matmul(a, w, lead=(), *, tm, tn, out_dtype=F32):
    m, k = a.shape
    kw, n = w.shape[-2:]
    assert k == kw and m % tm == 0 and n % tn == 0, (a.shape, w.shape, tm, tn)
    nlead = len(lead)
    out_bytes = jnp.dtype(out_dtype).itemsize
    window = tm * k * 2 + 2 * k * tn * 4 + 2 * tm * tn * out_bytes + k * tn * 2 + tm * tn * 4
    return pl.pallas_call(
        _mm_kernel,
        grid=(m // tm, n // tn),
        in_specs=[
            pl.BlockSpec((tm, k), lambda i, j: (i, 0), pipeline_mode=pl.Buffered(1)),
            pl.BlockSpec((None,) * nlead + (k, tn), lambda i, j: tuple(lead) + (0, j)),
        ],
        out_specs=pl.BlockSpec((tm, tn), lambda i, j: (i, j)),
        out_shape=jax.ShapeDtypeStruct((m, n), out_dtype),
        compiler_params=_params(("arbitrary", "arbitrary"), window),
        name="matmul",
    )(a, w)


def _ffn_up_kernel(h_ref, wg_ref, wu_ref, o_ref):
    h = h_ref[...]
    g = jnp.dot(h, wg_ref[...].astype(BF16), preferred_element_type=F32)
    u = jnp.dot(h, wu_ref[...].astype(BF16), preferred_element_type=F32)
    o_ref[...] = (_silu(g) * u).astype(o_ref.dtype)


def ffn_up(h, w_gu, lead, *, tm, tn):
    m, k = h.shape
    n2 = w_gu.shape[-1]
    n = n2 // 2
    assert m % tm == 0 and n % tn == 0
    nj = n // tn
    nlead = len(lead)
    window = tm * k * 2 + 2 * 2 * k * tn * 4 + 2 * tm * tn * 2 + 2 * k * tn * 2 + 3 * tm * tn * 4
    wblock = (None,) * nlead + (k, tn)
    return pl.pallas_call(
        _ffn_up_kernel,
        grid=(m // tm, nj),
        in_specs=[
            pl.BlockSpec((tm, k), lambda i, j: (i, 0), pipeline_mode=pl.Buffered(1)),
            pl.BlockSpec(wblock, lambda i, j: tuple(lead) + (0, j)),
            pl.BlockSpec(wblock, lambda i, j: tuple(lead) + (0, nj + j)),
        ],
        out_specs=pl.BlockSpec((tm, tn), lambda i, j: (i, j)),
        out_shape=jax.ShapeDtypeStruct((m, n), BF16),
        compiler_params=_params(("arbitrary", "arbitrary"), window),
        name="ffn_up",
    )(h, w_gu, w_gu)


ATTN_BQ = 256


def _rope(x, c, s):
    lane = lax.broadcasted_iota(jnp.int32, x.shape, 1)
    half = AXIS_DIM // 2
    partner = jnp.where(lane % AXIS_DIM < half, pltpu.roll(x, HEAD_DIM - half, 1), pltpu.roll(x, half, 1))
    return x * c + partner * s


def _softmax_pv(s, sink, v):
    m = jnp.maximum(jnp.max(s, axis=-1, keepdims=True), sink)
    e = jnp.exp(s - m)
    denom = jnp.sum(e, axis=-1, keepdims=True) + jnp.exp(sink - m)
    p = (e / denom).astype(BF16)
    return jnp.dot(p, v, preferred_element_type=F32)


def _qk(q, k):
    return lax.dot_general(q, k, (((1,), (1,)), ((), ())), preferred_element_type=F32)


def _attn_kernel(sink_ref, q_ref, kc_ref, vc_ref, kp_ref, vp_ref, kn_ref, vn_ref, kx_ref, vx_ref,
                 cq_ref, sq_ref, cp_ref, sp_ref, cn_ref, sn_ref, o_ref, *, lay):
    i = pl.program_id(0)
    kvh = pl.program_id(1)
    bq, w, hd = ATTN_BQ, ATTN_WINDOW, HEAD_DIM
    past = kx_ref.shape[0]

    @pl.when(i < lay.mc // bq)
    def _context():
        k = kc_ref[...].astype(BF16)
        v = vc_ref[...].astype(BF16)
        for g in range(ATTN_GROUP):
            q = q_ref[:, g * hd:(g + 1) * hd].astype(BF16)
            s = _qk(q, k) * ATTN_SCALE
            o_ref[:, g * hd:(g + 1) * hd] = _softmax_pv(s, sink_ref[kvh * ATTN_GROUP + g], v).astype(o_ref.dtype)

    @pl.when(i >= lay.mc // bq)
    def _latent():
        pos = lax.rem(jnp.maximum(i * bq - lay.mc, 0), lay.ll)
        far = 4 * bq
        no_prev = jnp.where(pos > 0, 0, far)
        no_next = jnp.where(pos + bq < lay.ll, 0, far)
        r_c = lax.broadcasted_iota(jnp.int32, (bq, bq), 0)
        c_c = lax.broadcasted_iota(jnp.int32, (bq, bq), 1)
        r_h = lax.broadcasted_iota(jnp.int32, (bq, w), 0)
        c_h = lax.broadcasted_iota(jnp.int32, (bq, w), 1)
        valid = [jnp.abs(r_c - c_c) <= w,
                 c_h >= r_h + no_prev,
                 c_h + no_next <= r_h - (bq - w)]
        bias = jnp.concatenate([jnp.where(t, 0.0, NEG_INF).astype(F32) for t in valid]
                               + [jnp.zeros((bq, past), F32)], axis=1)
        cq, sq = cq_ref[...], sq_ref[...]
        k = jnp.concatenate([_rope(kc_ref[...], cq, sq), _rope(kp_ref[...], cp_ref[...], sp_ref[...]),
                             _rope(kn_ref[...], cn_ref[...], sn_ref[...]), kx_ref[...]], axis=0).astype(BF16)
        v = jnp.concatenate([vc_ref[...], vp_ref[...], vn_ref[...], vx_ref[...]], axis=0).astype(BF16)
        for g in range(ATTN_GROUP):
            q = _rope(q_ref[:, g * hd:(g + 1) * hd], cq, sq).astype(BF16)
            s = _qk(q, k) * ATTN_SCALE + bias
            o_ref[:, g * hd:(g + 1) * hd] = _softmax_pv(s, sink_ref[kvh * ATTN_GROUP + g], v).astype(o_ref.dtype)


def rope_tables(n_tokens):
    rows = n_tokens // GRID_W
    r = jnp.repeat(jnp.arange(rows), GRID_W).astype(F32)
    col = jnp.tile(jnp.arange(GRID_W), rows).astype(F32)
    inv = ROPE_THETA ** (-jnp.arange(0, AXIS_DIM, 2, dtype=F32) / AXIS_DIM)
    ang = jnp.stack([r[:, None] * inv, col[:, None] * inv], axis=1)
    cos, sin = jnp.cos(ang), jnp.sin(ang)
    c = jnp.stack([cos, cos], axis=2).reshape(n_tokens, HEAD_DIM)
    s = jnp.stack([-sin, sin], axis=2).reshape(n_tokens, HEAD_DIM)
    return c, s


def attention(lay, qkv, sink, cache_k, cache_v, slot, rope_c, rope_s):
    m = qkv.shape[0]
    bq, w, hd = ATTN_BQ, ATTN_WINDOW, HEAD_DIM
    assert lay.lc == bq and lay.ll % bq == 0 and bq % w == 0
    past = cache_k.shape[2]
    qw = ATTN_GROUP * hd
    kcol, vcol = ATTN_NQ // hd, (ATTN_NQ + ATTN_NK) // hd
    per = bq // w

    def pos(i):
        return lax.rem(jnp.maximum(i * bq - lay.mc, 0), lay.ll)

    def prev_row(i):
        return jnp.maximum(i * per - 1, 0)

    def next_row(i):
        return jnp.minimum((i + 1) * per, m // w - 1)

    def cache_map(i, h):
        return (jnp.maximum(i * bq - lay.mc, 0) // lay.ll, slot, 0, h)

    tab_q = pl.BlockSpec((bq, hd), lambda i, h: (pos(i) // bq, 0))
    tab_p = pl.BlockSpec((w, hd), lambda i, h: (jnp.maximum(pos(i) - w, 0) // w, 0))
    tab_n = pl.BlockSpec((w, hd), lambda i, h: (jnp.minimum(pos(i) + bq, lay.ll - w) // w, 0))
    in_specs = [
        pl.BlockSpec(memory_space=pltpu.SMEM),
        pl.BlockSpec((bq, qw), lambda i, h: (i, h)),
        pl.BlockSpec((bq, hd), lambda i, h: (i, kcol + h)),
        pl.BlockSpec((bq, hd), lambda i, h: (i, vcol + h)),
        pl.BlockSpec((w, hd), lambda i, h: (prev_row(i), kcol + h)),
        pl.BlockSpec((w, hd), lambda i, h: (prev_row(i), vcol + h)),
        pl.BlockSpec((w, hd), lambda i, h: (next_row(i), kcol + h)),
        pl.BlockSpec((w, hd), lambda i, h: (next_row(i), vcol + h)),
        pl.BlockSpec((None, None, past, hd), cache_map),
        pl.BlockSpec((None, None, past, hd), cache_map),
        tab_q, tab_q, tab_p, tab_p, tab_n, tab_n,
    ]
    nk = bq + 2 * w + past
    window = 2 * 4 * (bq * qw + 2 * (bq + 2 * w + past) * hd + 6 * bq * hd) + 2 * bq * qw * 2
    work = 4 * bq * nk * 4 + 4 * nk * hd * 2
    return pl.pallas_call(
        functools.partial(_attn_kernel, lay=lay),
        grid=(m // bq, ATTN_KV_HEADS),
        in_specs=in_specs,
        out_specs=pl.BlockSpec((bq, qw), lambda i, h: (i, h)),
        out_shape=jax.ShapeDtypeStruct((m, ATTN_NQ), BF16),
        compiler_params=_params(("arbitrary", "arbitrary"), window + work),
        name="attention",
    )(sink, qkv, qkv, qkv, qkv, qkv, qkv, qkv, cache_k, cache_v, rope_c, rope_s, rope_c, rope_s, rope_c, rope_s)


def _split3(x):
    hi = x.astype(BF16)
    r = x - hi.astype(F32)
    mid = r.astype(BF16)
    lo = (r - mid.astype(F32)).astype(BF16)
    return hi, mid, lo


def _select_rows(sel, x):
    return sum(jnp.dot(sel, p, preferred_element_type=F32) for p in _split3(x))


def _tri(q, fwd):
    r = lax.broadcasted_iota(jnp.int32, (q, q), 0)
    c = lax.broadcasted_iota(jnp.int32, (q, q), 1)
    return (r >= c) if fwd else (r <= c)


CONV_TR = 256
CONV_TC = 1024
CONV_HALO = V7X_SUBLANES


def _ssd_conv_kernel(x_ref, xp_ref, xn_ref, w_ref, b_ref, o_ref, *, lay):
    tr = x_ref.shape[0]
    row0 = pl.program_id(0) * tr
    in_lat = row0 >= lay.mc
    pos = jnp.where(in_lat, lax.rem(jnp.maximum(row0 - lay.mc, 0), lay.ll), lax.rem(row0, lay.lc))
    seq_len = jnp.where(in_lat, lay.ll, lay.lc)
    xp = jnp.where(pos > 0, xp_ref[...], 0.0)
    xn = jnp.where(pos + tr < seq_len, xn_ref[...], 0.0)
    ext = jnp.concatenate([xp, x_ref[...], xn], axis=0)
    n = tr + 2 * CONV_HALO
    acc = x_ref[...] * w_ref[SSD_CONV // 2:SSD_CONV // 2 + 1, :] + b_ref[...]
    for k in range(SSD_CONV):
        d = k - SSD_CONV // 2
        if d != 0:
            tap = pltpu.roll(ext, (n - d) % n, 0)[CONV_HALO:CONV_HALO + tr]
            acc = acc + tap * w_ref[k:k + 1, :]
    o_ref[...] = _silu(acc)


def ssd_conv(lay, proj, conv_w, conv_b, slot):
    m = proj.shape[0]
    tr, tc, halo = CONV_TR, CONV_TC, CONV_HALO
    lay.check_tile(tr)
    assert lay.lc % tr == 0 and SSD_INNER % tc == 0 and SSD_CONV_CH % tc == 0
    col0 = SSD_INNER // tc
    per = tr // halo
    n_slots, _, ch = conv_w.shape
    return pl.pallas_call(
        functools.partial(_ssd_conv_kernel, lay=lay),
        grid=(m // tr, ch // tc),
        in_specs=[
            pl.BlockSpec((tr, tc), lambda i, j: (i, col0 + j)),
            pl.BlockSpec((halo, tc), lambda i, j: (jnp.maximum(i * per - 1, 0), col0 + j)),
            pl.BlockSpec((halo, tc), lambda i, j: (jnp.minimum((i + 1) * per, m // halo - 1), col0 + j)),
            pl.BlockSpec((None, SSD_CONV, tc), lambda i, j: (slot, 0, j)),
            pl.BlockSpec((None, 1, tc), lambda i, j: (slot, 0, j)),
        ],
        out_specs=pl.BlockSpec((tr, tc), lambda i, j: (i, j)),
        out_shape=jax.ShapeDtypeStruct((m, ch), F32),
        compiler_params=_params(("arbitrary", "arbitrary"), 12 * tr * tc * 4),
        name="ssd_conv",
    )(proj, proj, proj, conv_w, conv_b.reshape(n_slots, 1, ch))


SSD_PAIR = 2 * SSD_HEAD_DIM
SSD_GROUP_W = SSD_GROUP_HEADS * SSD_HEAD_DIM


SSD_NG = 4


def _ssd_scan_kernel(*refs, lay, has_h0):
    refs = list(refs)
    dirs = [tuple(refs.pop(0) for _ in range(4)) for _ in range(2)]
    bias_ref, alog_ref = refs.pop(0), refs.pop(0)
    h0_ref = refs.pop(0) if has_h0 else None
    y_refs = (refs.pop(0), refs.pop(0))
    st_ref = refs.pop(0)
    ht_ref, acum_ref, rows_ref, tr_ref = refs
    g0 = pl.program_id(0) * SSD_NG
    q, n_pairs, gw, pw = SSD_CHUNK, SSD_GROUP_W // SSD_PAIR, SSD_GROUP_W, SSD_PAIR
    is_lat, _, c, nc = lay.chunk(pl.program_id(1), q)

    def state_rows(gi, pr):
        return pl.ds(pl.multiple_of(gi * gw + pr * pw, pw), pw)

    @pl.when(c == 0)
    def _init():
        def body(gi, carry):
            for d in range(2):
                for pr in range(n_pairs):
                    if has_h0:
                        init = jnp.where(is_lat, h0_ref[d, state_rows(gi, pr), :].T, 0.0)
                    else:
                        init = jnp.zeros((SSD_STATE, pw), F32)
                    ht_ref[d, gi, :, pr * pw:(pr + 1) * pw] = init
            return carry
        lax.fori_loop(0, SSD_NG, body, 0)

    tris = (_tri(q, True), _tri(q, False))
    first_head = lax.broadcasted_iota(jnp.int32, (SSD_HEADS // 2, pw), 1) < q
    for d in range(2):
        dt = jax.nn.softplus(dirs[d][3][...] + bias_ref[d:d + 1, :])
        a = dt * (-jnp.exp(alog_ref[d:d + 1, :]))
        acum = _select_rows(tris[d].astype(BF16), a)
        acum_ref[d] = acum
        total = acum[q - 1:q, :] if d == 0 else acum[0:1, :]
        to_end = jnp.exp(total - acum) * dt
        for k, v in enumerate((acum, dt, to_end)):
            tr_ref[d, k] = jnp.concatenate([v, v], axis=0).T
            even = tr_ref[d, k, pl.ds(0, SSD_HEADS // 2, stride=2), :]
            odd = tr_ref[d, k, pl.ds(1, SSD_HEADS // 2, stride=2), :]
            rows_ref[d, k] = jnp.where(first_head, even, odd)

    def group_body(gi, carry):
        low = lax.broadcasted_iota(jnp.int32, (q, pw), 1) < SSD_HEAD_DIM
        r2 = lax.broadcasted_iota(jnp.int32, (q, pw), 0)
        c2 = lax.rem(lax.broadcasted_iota(jnp.int32, (q, pw), 1), q)
        tri2 = (r2 >= c2, r2 <= c2)
        same_head = ((lax.broadcasted_iota(jnp.int32, (2 * q, pw), 0) >= q)
                     == (lax.broadcasted_iota(jnp.int32, (2 * q, pw), 1) >= SSD_HEAD_DIM))
        gsel = g0 + gi
        shift = lax.rem(V7X_LANES - gsel * SSD_GROUP_HEADS, V7X_LANES)
        gcol = pl.ds(pl.multiple_of(gi * SSD_STATE, SSD_STATE), SSD_STATE)
        for d, (x_ref, b_ref, c_ref, _) in enumerate(dirs):
            acum_g = pltpu.roll(acum_ref[d], shift, 1)
            b32 = b_ref[:, gcol]
            bm = b32.astype(BF16)
            cm = c_ref[:, gcol].astype(BF16)
            cb2 = _qk(cm, jnp.concatenate([bm, bm], axis=0))
            bt2 = jnp.concatenate([b32, b32], axis=0).T
            for pr in range(n_pairs):
                r0 = 2 * pr
                xcol = state_rows(gi, pr)
                blk = slice(pr * pw, (pr + 1) * pw)
                row = pl.ds(gsel * n_pairs + pr, 1)
                ac = jnp.where(low, acum_g[:, r0:r0 + 1], acum_g[:, r0 + 1:r0 + 2])
                ar, dtr, ter = rows_ref[d, 0, row, :], rows_ref[d, 1, row, :], rows_ref[d, 2, row, :]
                total = ac[q - 1:q, :] if d == 0 else ac[0:1, :]
                w = (cb2 * jnp.exp(jnp.where(tri2[d], ac - ar, -jnp.inf)) * dtr).astype(BF16)
                xp = x_ref[:, xcol]
                xbd = jnp.where(same_head, jnp.concatenate([xp, xp], axis=0), 0.0).astype(BF16)
                ht = ht_ref[d, gi, :, blk]
                y = jnp.dot(w, xbd, preferred_element_type=F32)
                y = y + jnp.dot(cm, ht.astype(BF16), preferred_element_type=F32) * jnp.exp(ac)
                y_refs[d][:, xcol] = y
                bw = (bt2 * ter).astype(BF16)
                ht_ref[d, gi, :, blk] = ht * jnp.exp(total) + jnp.dot(bw, xbd, preferred_element_type=F32)
        return carry

    lax.fori_loop(0, SSD_NG, group_body, 0, unroll=2)

    @pl.when(jnp.logical_and(c == nc - 1, jnp.logical_not(is_lat)))
    def _emit():
        def body(gi, carry):
            for d in range(2):
                for pr in range(n_pairs):
                    st_ref[d, state_rows(gi, pr), :] = ht_ref[d, gi, :, pr * pw:(pr + 1) * pw].T
            return carry
        lax.fori_loop(0, SSD_NG, body, 0)


def ssd_scan(lay, proj, xbc, dt_bias, a_log, slot, h0):
    m = proj.shape[0]
    q, ng = SSD_CHUNK, SSD_NG
    gw, st = SSD_GROUP_W, SSD_STATE
    assert SSD_HEADS == V7X_LANES and SSD_GROUPS % ng == 0 and 2 * q == SSD_PAIR
    b_col, c_col = SSD_INNER // (ng * st), (SSD_INNER + SSD_GN) // (ng * st)
    dt_col = (SSD_INNER + SSD_CONV_CH) // V7X_LANES

    def row_f(t):
        return t

    def row_b(t):
        return lay.mirror(t, q)

    def dir_specs(row, d):
        return [
            pl.BlockSpec((q, ng * gw), lambda g, t: (row(t), g)),
            pl.BlockSpec((q, ng * st), lambda g, t: (row(t), b_col + g)),
            pl.BlockSpec((q, ng * st), lambda g, t: (row(t), c_col + g)),
            pl.BlockSpec((q, V7X_LANES), lambda g, t: (row(t), dt_col + d)),
        ]

    def seq_ctx(t):
        is_lat, seq, _, _ = lay.chunk(t, q)
        return jnp.where(is_lat, lay.bc - 1, seq)

    def seq_lat(t):
        is_lat, seq, _, _ = lay.chunk(t, q)
        return jnp.where(is_lat, seq, 0)

    par_spec = pl.BlockSpec((None, 2, SSD_HEADS), lambda g, t: (slot, 0, 0))
    in_specs = dir_specs(row_f, 0) + dir_specs(row_b, 1) + [par_spec, par_spec]
    args = [xbc, xbc, xbc, proj, xbc, xbc, xbc, proj, dt_bias, a_log]
    has_h0 = h0 is not None
    if has_h0:
        in_specs.append(pl.BlockSpec((None, None, 2, ng * gw, st), lambda g, t: (seq_lat(t), slot, 0, g, 0)))
        args.append(h0)
    y_shape = jax.ShapeDtypeStruct((m, SSD_INNER), F32)
    state_bytes = 2 * ng * gw * st * 4
    window = 2 * 4 * (2 * 2 * q * ng * gw + 4 * q * ng * st) + 5 * state_bytes
    return pl.pallas_call(
        functools.partial(_ssd_scan_kernel, lay=lay, has_h0=has_h0),
        grid=(SSD_GROUPS // ng, lay.n_chunks(q)),
        in_specs=in_specs,
        out_specs=[
            pl.BlockSpec((q, ng * gw), lambda g, t: (row_f(t), g)),
            pl.BlockSpec((q, ng * gw), lambda g, t: (row_b(t), g)),
            pl.BlockSpec((None, 2, ng * gw, st), lambda g, t: (seq_ctx(t), 0, g, 0)),
        ],
        out_shape=[y_shape, y_shape, jax.ShapeDtypeStruct((lay.bc, 2, SSD_INNER, st), F32)],
        scratch_shapes=[pltpu.VMEM((2, ng, st, gw), F32), pltpu.VMEM((2, q, SSD_HEADS), F32),
                        pltpu.VMEM((2, 3, SSD_HEADS // 2, SSD_PAIR), F32),
                        pltpu.VMEM((2, 3, SSD_HEADS, SSD_PAIR), F32)],
        compiler_params=_params(("arbitrary", "arbitrary"), window),
        name="ssd_scan",
    )(*args)


SSD_GATE_TM = 256


def _ssd_gate_kernel(yf_ref, yb_ref, x_ref, z_ref, d_ref, g_ref, o_ref):
    y = (yf_ref[...] + yb_ref[...] + d_ref[...] * x_ref[...]) * _silu(z_ref[...])
    o_ref[...] = _rmsnorm(y, g_ref[...]).astype(o_ref.dtype)


def ssd_gate(lay, y_f, y_b, xbc, proj, d_skip, norm_g, slot):
    m = proj.shape[0]
    tm, gw = SSD_GATE_TM, SSD_INNER // SSD_GROUPS
    n_slots = norm_g.shape[0]
    d_cols = jnp.repeat(d_skip[slot], SSD_HEAD_DIM).reshape(1, SSD_INNER)
    blk = pl.BlockSpec((tm, gw), lambda i, g: (i, g))
    return pl.pallas_call(
        _ssd_gate_kernel,
        grid=(m // tm, SSD_GROUPS),
        in_specs=[blk, blk, blk, blk,
                  pl.BlockSpec((1, gw), lambda i, g: (0, g)),
                  pl.BlockSpec((None, 1, gw), lambda i, g: (slot, 0, g))],
        out_specs=blk,
        out_shape=jax.ShapeDtypeStruct((m, SSD_INNER), BF16),
        compiler_params=_params(("arbitrary", "arbitrary"), 16 * tm * gw * 4),
        name="ssd_gate",
    )(y_f, y_b, xbc, proj, d_cols, norm_g.reshape(n_slots, 1, SSD_INNER))


HGRN_HB = 8
HGRN_W = HGRN_HB * HGRN_DK


def _hgrn_scan_kernel(*refs, lay, layer, has_s0):
    refs = list(refs)
    dirs = [tuple(refs.pop(0) for _ in range(3)) for _ in range(2)]
    lb_ref = refs.pop(0)
    s0_ref = refs.pop(0) if has_s0 else None
    y_refs = (refs.pop(0), refs.pop(0))
    st_ref, s_ref = refs.pop(0), refs.pop(0)
    q = HGRN_CHUNK
    is_lat, _, c, nc = lay.chunk(pl.program_id(1), q)

    dk, dv = HGRN_DK, HGRN_DV

    @pl.when(c == 0)
    def _init():
        for d in range(2):
            for pp in range(HGRN_HB // 2):
                s_ref[d, pp] = jnp.zeros((2 * dv, 2 * dk), F32)
                if has_s0:
                    for h in range(2):
                        s_ref[d, pp, h * dv:(h + 1) * dv, h * dk:(h + 1) * dk] = jnp.where(
                            is_lat, s0_ref[d, 2 * pp + h].T, 0.0)

    logits = lb_ref[...]
    e = jnp.exp(logits - jnp.max(logits, axis=0, keepdims=True))
    p = e / jnp.sum(e, axis=0, keepdims=True)
    lb = jnp.zeros(p.shape[1:], F32)
    for l in range(1, layer + 1):
        lb = lb + p[l]

    for d, (q_ref, v_ref, f_ref) in enumerate(dirs):
        fwd = d == 0
        tri = _tri(q, fwd)
        lbd = lb[d:d + 1, :]
        fr = f_ref[...]
        f = lbd + (1.0 - lbd) * jax.nn.sigmoid(fr)
        k = (1.0 - lbd) * jax.nn.sigmoid(-fr)
        b = _select_rows(tri.astype(BF16), jnp.log(f))
        total = b[q - 1:q, :] if fwd else b[0:1, :]
        qd = (_silu(q_ref[...]) * jnp.exp(b)).astype(BF16)
        kd = (k * jnp.exp(-b)).astype(BF16)
        ks = (k * jnp.exp(total - b)).astype(BF16)
        dec = jnp.exp(total)
        r2 = lax.broadcasted_iota(jnp.int32, (q, 2 * q), 0)
        c2 = lax.rem(lax.broadcasted_iota(jnp.int32, (q, 2 * q), 1), q)
        tri2 = (r2 >= c2) if fwd else (r2 <= c2)
        same_head = ((lax.broadcasted_iota(jnp.int32, (2 * q, 2 * dk), 0) >= q)
                     == (lax.broadcasted_iota(jnp.int32, (2 * q, 2 * dk), 1) >= dk))

        def pair_diag(x):
            return jnp.where(same_head, jnp.concatenate([x, x], axis=0), jnp.zeros((), x.dtype))

        for pp in range(HGRN_HB // 2):
            sl = slice(pp * 2 * dk, (pp + 1) * 2 * dk)
            v_bd = pair_diag(v_ref[:, sl].astype(BF16))
            att = jnp.where(tri2, _qk(qd[:, sl], pair_diag(kd[:, sl])), 0.0).astype(BF16)
            s_prev = s_ref[d, pp]
            o = jnp.dot(att, v_bd, preferred_element_type=F32) + _qk(qd[:, sl], s_prev.astype(BF16))
            y_refs[d][:, sl] = o
            s_ref[d, pp] = s_prev * dec[:, sl] + jnp.dot(v_bd.T, pair_diag(ks[:, sl]), preferred_element_type=F32)

    @pl.when(jnp.logical_and(c == nc - 1, jnp.logical_not(is_lat)))
    def _emit():
        for d in range(2):
            for hh in range(HGRN_HB):
                h = hh % 2
                st_ref[d, hh] = s_ref[d, hh // 2, h * dv:(h + 1) * dv, h * dk:(h + 1) * dk].T


def hgrn_scan(lay, proj, lb_logits, layer, slot, s0):
    m = proj.shape[0]
    q, w = HGRN_CHUNK, HGRN_W
    v_col, f_col = HGRN_HK // w, (HGRN_HK + HGRN_HV) // w
    fb_col = f_col + HGRN_HK // w

    def row_f(t):
        return t

    def row_b(t):
        return lay.mirror(t, q)

    def dir_specs(row, fcol):
        return [
            pl.BlockSpec((q, w), lambda hb, t: (row(t), hb)),
            pl.BlockSpec((q, w), lambda hb, t: (row(t), v_col + hb)),
            pl.BlockSpec((q, w), lambda hb, t: (row(t), fcol + hb)),
        ]

    def seq_ctx(t):
        is_lat, seq, _, _ = lay.chunk(t, q)
        return jnp.where(is_lat, lay.bc - 1, seq)

    def seq_lat(t):
        is_lat, seq, _, _ = lay.chunk(t, q)
        return jnp.where(is_lat, seq, 0)

    in_specs = dir_specs(row_f, f_col) + dir_specs(row_b, fb_col)
    in_specs.append(pl.BlockSpec((DEPTH, 2, w), lambda hb, t: (0, 0, hb)))
    args = [proj] * 6 + [lb_logits]
    has_s0 = s0 is not None
    if has_s0:
        in_specs.append(pl.BlockSpec((None, None, 2, HGRN_HB, HGRN_DK, HGRN_DV), lambda hb, t: (seq_lat(t), slot, 0, hb, 0, 0)))
        args.append(s0)
    y_shape = jax.ShapeDtypeStruct((m, HGRN_HV), F32)
    state_bytes = 2 * HGRN_HB * HGRN_DK * HGRN_DV * 4
    window = 2 * 4 * (8 * q * w) + 5 * state_bytes + 16 * q * w * 4
    return pl.pallas_call(
        functools.partial(_hgrn_scan_kernel, lay=lay, layer=layer, has_s0=has_s0),
        grid=(HGRN_HEADS // HGRN_HB, lay.n_chunks(q)),
        in_specs=in_specs,
        out_specs=[
            pl.BlockSpec((q, w), lambda hb, t: (row_f(t), hb)),
            pl.BlockSpec((q, w), lambda hb, t: (row_b(t), hb)),
            pl.BlockSpec((None, 2, HGRN_HB, HGRN_DK, HGRN_DV), lambda hb, t: (seq_ctx(t), 0, hb, 0, 0)),
        ],
        out_shape=[y_shape, y_shape, jax.ShapeDtypeStruct((lay.bc, 2, HGRN_HEADS, HGRN_DK, HGRN_DV), F32)],
        scratch_shapes=[pltpu.VMEM((2, HGRN_HB // 2, 2 * HGRN_DV, 2 * HGRN_DK), F32)],
        compiler_params=_params(("arbitrary", "arbitrary"), window),
        name="hgrn_scan",
    )(*args)


HGRN_NORM_TM = 256


def _hgrn_norm_kernel(yf_ref, yb_ref, g_ref, ng_ref, o_ref):
    ng = ng_ref[...]
    for hh in range(HGRN_HEADS):
        sl = slice(hh * HGRN_DV, (hh + 1) * HGRN_DV)
        o = yf_ref[:, sl] + yb_ref[:, sl]
        o_ref[:, sl] = (_rmsnorm(o, ng) * _silu(g_ref[:, sl])).astype(o_ref.dtype)


def hgrn_norm(lay, y_f, y_b, proj, norm_g, slot):
    m = proj.shape[0]
    tm = HGRN_NORM_TM
    n_slots = norm_g.shape[0]
    g_col = (3 * HGRN_HK + HGRN_HV) // HGRN_HV
    blk = pl.BlockSpec((tm, HGRN_HV), lambda i: (i, 0))
    return pl.pallas_call(
        _hgrn_norm_kernel,
        grid=(m // tm,),
        in_specs=[blk, blk, pl.BlockSpec((tm, HGRN_HV), lambda i: (i, g_col)),
                  pl.BlockSpec((None, 1, HGRN_DV), lambda i: (slot, 0, 0))],
        out_specs=blk,
        out_shape=jax.ShapeDtypeStruct((m, HGRN_HV), BF16),
        compiler_params=_params(("arbitrary",), 10 * tm * HGRN_HV * 4),
        name="hgrn_norm",
    )(y_f, y_b, proj, norm_g.reshape(n_slots, 1, HGRN_DV))


MM_TM = 2048
DEEP_TM = 1024
FFN_TN = 256
PROJ_TN = 512
SSD_IN_TN = 256


def kernel(x_prompt, x_sample, cache_attn_k, cache_attn_v, state_ssd, state_hgrn, c, c_ctx, mod_w, mod_b, norm_g, ffn_w_gu, ffn_w_down, attn_w_in, attn_sink, attn_w_out, ssd_w_in, ssd_conv_w, ssd_conv_b, ssd_dt_bias, ssd_A_log, ssd_D, ssd_norm_g, ssd_w_out, hgrn_w_in, hgrn_lb_logits, hgrn_norm_g, hgrn_w_out):
    bc, lc, d = x_prompt.shape
    bl, ll, _ = x_sample.shape
    lay = Layout(bc, lc, bl, ll)
    lay.check_tile(MM_TM)
    assert d == D_MODEL and 1 + bl <= MOD_ROWS
    mc = lay.mc
    x = jnp.concatenate([x_prompt.reshape(mc, d), x_sample.reshape(bl * ll, d)], axis=0)

    cond = jnp.concatenate([c_ctx[None, :], c, jnp.zeros((MOD_ROWS - 1 - bl, d), F32)], axis=0)
    mod_rows = modulation_all(cond, mod_w, mod_b).reshape(DEPTH * MOD_ROWS * N_MOD, 1, d)
    norm_rows = norm_g.reshape(DEPTH * 6, 1, d)
    rope_c, rope_s = rope_tables(ll)
    past = cache_attn_k.shape[2]
    cache_k = cache_attn_k.reshape(bl, -1, past, ATTN_NK)
    cache_v = cache_attn_v.reshape(bl, -1, past, ATTN_NK)
    ssd_h0 = state_ssd.reshape(bl, -1, 2, SSD_INNER, SSD_STATE)

    def ffn(h, layer, half):
        a = ffn_up(h, ffn_w_gu, (layer, half), tm=MM_TM, tn=FFN_TN)
        return matmul(a, ffn_w_down, (layer, half), tm=DEEP_TM, tn=FFN_TN)

    new_k, new_v, new_ssd, new_hgrn = [], [], [], []
    _, h = ada_step(lay, x, mod_rows, norm_rows, inn=(0, 0))
    for layer in range(DEPTH):
        x, h = ada_step(lay, x, mod_rows, norm_rows, out=(layer, 0, FFN_RES_W, ffn(h, layer, 0)), inn=(layer, 1))
        kind, slot = layer % N_MIXERS, layer // N_MIXERS
        if kind == 0:
            qkv = matmul(h, attn_w_in, (slot,), tm=MM_TM, tn=PROJ_TN)
            new_k.append(qkv[:mc, ATTN_NQ:ATTN_NQ + ATTN_NK].reshape(bc, lc, ATTN_KV_HEADS, HEAD_DIM))
            new_v.append(qkv[:mc, ATTN_NQ + ATTN_NK:].reshape(bc, lc, ATTN_KV_HEADS, HEAD_DIM))
            o = attention(lay, qkv, attn_sink[slot], cache_k, cache_v, slot, rope_c, rope_s)
            y = matmul(o, attn_w_out, (slot,), tm=MM_TM, tn=PROJ_TN)
        elif kind == 1:
            proj = matmul(h, ssd_w_in, (slot,), tm=MM_TM, tn=SSD_IN_TN)
            xbc = ssd_conv(lay, proj, ssd_conv_w, ssd_conv_b, slot)
            y_f, y_b, st = ssd_scan(lay, proj, xbc, ssd_dt_bias, ssd_A_log, slot, ssd_h0)
            new_ssd.append(st.reshape(bc, 2, SSD_HEADS, SSD_HEAD_DIM, SSD_STATE))
            o = ssd_gate(lay, y_f, y_b, xbc, proj, ssd_D, ssd_norm_g, slot)
            y = matmul(o, ssd_w_out, (slot,), tm=DEEP_TM, tn=FFN_TN)
        else:
            proj = matmul(h, hgrn_w_in, (slot,), tm=MM_TM, tn=PROJ_TN)
            y_f, y_b, st = hgrn_scan(lay, proj, hgrn_lb_logits, layer, slot, state_hgrn)
            new_hgrn.append(st)
            o = hgrn_norm(lay, y_f, y_b, proj, hgrn_norm_g, slot)
            y = matmul(o, hgrn_w_out, (slot,), tm=MM_TM, tn=PROJ_TN)
        x, h = ada_step(lay, x, mod_rows, norm_rows, out=(layer, 1, 1.0, y), inn=(layer, 2))
        nxt = (layer + 1, 0) if layer + 1 < DEPTH else None
        x, h = ada_step(lay, x, mod_rows, norm_rows, out=(layer, 2, FFN_RES_W, ffn(h, layer, 1)), inn=nxt)

    return (x[:mc].reshape(bc, lc, d), x[mc:].reshape(bl, ll, d),
            jnp.stack(new_k, axis=1), jnp.stack(new_v, axis=1),
            jnp.stack(new_ssd, axis=1).astype(state_ssd.dtype), jnp.stack(new_hgrn, axis=1).astype(state_hgrn.dtype))
```

```python
import functools
import math
from typing import NamedTuple

import jax
import jax.numpy as jnp
from jax import lax
from jax.experimental import pallas as pl
from jax.experimental.pallas import tpu as pltpu

F32 = jnp.float32
BF16 = jnp.bfloat16

D_MODEL = 4096
DEPTH = 4
GRID_W = 64
N_MIXERS = 3
ATTN_HEADS = 32
ATTN_KV_HEADS = 8
ATTN_GROUP = ATTN_HEADS // ATTN_KV_HEADS
HEAD_DIM = D_MODEL // ATTN_HEADS
AXIS_DIM = HEAD_DIM // 2
ATTN_WINDOW = 128
ATTN_SCALE = HEAD_DIM ** -0.5
ROPE_THETA = 10000.0
ATTN_NQ = ATTN_HEADS * HEAD_DIM
ATTN_NK = ATTN_KV_HEADS * HEAD_DIM
ATTN_IN = ATTN_NQ + 2 * ATTN_NK
SSD_INNER = 2 * D_MODEL
SSD_HEAD_DIM = 64
SSD_HEADS = SSD_INNER // SSD_HEAD_DIM
SSD_GROUPS = 8
SSD_GROUP_HEADS = SSD_HEADS // SSD_GROUPS
SSD_STATE = 128
SSD_GN = SSD_GROUPS * SSD_STATE
SSD_CONV = 5
SSD_CONV_CH = SSD_INNER + 2 * SSD_GN
SSD_CHUNK = 64
SSD_IN = SSD_INNER + SSD_CONV_CH + 2 * SSD_HEADS
HGRN_DK = 128
HGRN_HEADS = D_MODEL // HGRN_DK
HGRN_DV = D_MODEL // HGRN_HEADS
HGRN_HK = HGRN_HEADS * HGRN_DK
HGRN_HV = HGRN_HEADS * HGRN_DV
HGRN_CHUNK = 64
HGRN_IN = 3 * HGRN_HK + 2 * HGRN_HV
D_FF = 11008
FFN_RES_W = 0.5
N_MOD = 9
EPS = 1e-6
NEG_INF = -1e30

V7X_VMEM_BYTES = 64 * 1024 * 1024
V7X_SUBLANES = 8
V7X_LANES = 128
VMEM_LIMIT_CAP = V7X_VMEM_BYTES - 6 * 1024 * 1024
MOD_ROWS = V7X_SUBLANES


def _vmem_limit(buffer_bytes):
    return int(min(VMEM_LIMIT_CAP, buffer_bytes + buffer_bytes // 4 + (8 << 20)))


def _params(semantics, buffer_bytes):
    return pltpu.CompilerParams(dimension_semantics=semantics, vmem_limit_bytes=_vmem_limit(buffer_bytes))


class Layout(NamedTuple):
    bc: int
    lc: int
    bl: int
    ll: int

    @property
    def mc(self):
        return self.bc * self.lc

    @property
    def m(self):
        return self.bc * self.lc + self.bl * self.ll

    def check_tile(self, tm):
        assert self.mc % tm == 0 and self.ll % tm == 0, (self, tm)

    def group(self, i, tm):
        row = i * tm
        return jnp.where(row < self.mc, 0, 1 + (row - self.mc) // self.ll)

    def n_chunks(self, q):
        assert self.lc % q == 0 and self.ll % q == 0
        return self.mc // q + self.bl * (self.ll // q)

    def chunk(self, t, q):
        ncc, ncl, nctx = self.lc // q, self.ll // q, self.mc // q
        is_lat = t >= nctx
        u = jnp.maximum(t - nctx, 0)
        seq = jnp.where(is_lat, u // ncl, t // ncc)
        c = jnp.where(is_lat, lax.rem(u, ncl), lax.rem(t, ncc))
        n = jnp.where(is_lat, ncl, ncc)
        return is_lat, seq, c, n

    def mirror(self, t, q):
        _, _, c, n = self.chunk(t, q)
        return t + n - 1 - 2 * c


def _silu(x):
    return x * jax.nn.sigmoid(x)


def _rmsnorm(x, g):
    ms = jnp.mean(x * x, axis=-1, keepdims=True)
    return x * lax.rsqrt(ms + EPS) * g


MOD_TN = 512


def _mod_kernel(cond_ref, w_ref, b_ref, o_ref):
    s = _silu(cond_ref[...]).astype(BF16)
    w = w_ref[...].astype(BF16)
    o_ref[...] = jnp.dot(s, w, preferred_element_type=F32) + b_ref[...]


def modulation_all(cond, mod_w, mod_b):
    depth, d, n = mod_w.shape
    assert n % MOD_TN == 0
    window = 2 * (d * MOD_TN * 4) + MOD_ROWS * d * 4 + 4 * MOD_ROWS * MOD_TN * 4
    return pl.pallas_call(
        _mod_kernel,
        grid=(depth, n // MOD_TN),
        in_specs=[
            pl.BlockSpec((MOD_ROWS, d), lambda l, j: (0, 0)),
            pl.BlockSpec((None, d, MOD_TN), lambda l, j: (l, 0, j)),
            pl.BlockSpec((None, 1, MOD_TN), lambda l, j: (l, 0, j)),
        ],
        out_specs=pl.BlockSpec((None, MOD_ROWS, MOD_TN), lambda l, j: (l, 0, j)),
        out_shape=jax.ShapeDtypeStruct((depth, MOD_ROWS, n), F32),
        compiler_params=_params(("arbitrary", "arbitrary"), window),
        name="modulation",
    )(cond, mod_w, mod_b.reshape(depth, 1, n))


ADA_TM = 256


def _ada_kernel(*refs, has_y, has_h, res_w):
    refs = list(refs)
    x_ref = refs.pop(0)
    if has_y:
        y_ref, gate_ref, gout_ref = refs.pop(0), refs.pop(0), refs.pop(0)
    if has_h:
        gin_ref, scale_ref, shift_ref = refs.pop(0), refs.pop(0), refs.pop(0)
    x = x_ref[...]
    if has_y:
        xo_ref = refs.pop(0)
        x = x + (res_w * gate_ref[...]) * _rmsnorm(y_ref[...], gout_ref[...])
        xo_ref[...] = x
    if has_h:
        h_ref = refs.pop(0)
        h = _rmsnorm(x, gin_ref[...]) * (1.0 + scale_ref[...]) + shift_ref[...]
        h_ref[...] = h.astype(h_ref.dtype)


def ada_step(lay, x, mod_rows, norm_rows, *, out=None, inn=None):
    m, d = x.shape
    tm = ADA_TM
    lay.check_tile(tm)
    has_y, has_h = out is not None, inn is not None
    res_w = out[2] if has_y else 1.0

    def mod_spec(layer, k):
        return pl.BlockSpec((None, 1, d), lambda i: ((layer * MOD_ROWS + lay.group(i, tm)) * N_MOD + k, 0, 0))

    def norm_spec(layer, k):
        return pl.BlockSpec((None, 1, d), lambda i: (layer * 6 + k, 0, 0))

    row_spec = pl.BlockSpec((tm, d), lambda i: (i, 0))
    args, in_specs, out_shape, out_specs = [x], [row_spec], [], []
    window = tm * d * 4
    if has_y:
        layer, j, _, y = out
        args += [y, mod_rows, norm_rows]
        in_specs += [row_spec, mod_spec(layer, 3 * j + 2), norm_spec(layer, 2 * j + 1)]
        out_shape.append(jax.ShapeDtypeStruct((m, d), F32))
        out_specs.append(row_spec)
        window += 2 * tm * d * 4
    if has_h:
        layer, j = inn
        args += [norm_rows, mod_rows, mod_rows]
        in_specs += [norm_spec(layer, 2 * j), mod_spec(layer, 3 * j + 1), mod_spec(layer, 3 * j)]
        out_shape.append(jax.ShapeDtypeStruct((m, d), BF16))
        out_specs.append(row_spec)
        window += tm * d * 2
    outs = pl.pallas_call(
        functools.partial(_ada_kernel, has_y=has_y, has_h=has_h, res_w=res_w),
        grid=(m // tm,),
        in_specs=in_specs,
        out_specs=out_specs,
        out_shape=out_shape,
        compiler_params=_params(("arbitrary",), 2 * window + 4 * tm * d * 4),
        name="ada_step",
    )(*args)
    outs = list(outs)
    x_new = outs.pop(0) if has_y else None
    h = outs.pop(0) if has_h else None
    return x_new, h


def _mm_kernel(a_ref, w_ref, o_ref):
    o_ref[...] = jnp.dot(a_ref[...], w_ref[...].astype(BF16), preferred_element_type=F32).astype(o_ref.dtype)


def matmul(a, w, lead=(), *, tm, tn, out_dtype=F32):
    m, k = a.shape
    kw, n = w.shape[-2:]
    assert k == kw and m % tm == 0 and n % tn == 0, (a.shape, w.shape, tm, tn)
    nlead = len(lead)
    out_bytes = jnp.dtype(out_dtype).itemsize
    window = tm * k * 2 + 2 * k * tn * 4 + 2 * tm * tn * out_bytes + k * tn * 2 + tm * tn * 4
    return pl.pallas_call(
        _mm_kernel,
        grid=(m // tm, n // tn),
        in_specs=[
            pl.BlockSpec((tm, k), lambda i, j: (i, 0), pipeline_mode=pl.Buffered(1)),
            pl.BlockSpec((None,) * nlead + (k, tn), lambda i, j: tuple(lead) + (0, j)),
        ],
        out_specs=pl.BlockSpec((tm, tn), lambda i, j: (i, j)),
        out_shape=jax.ShapeDtypeStruct((m, n), out_dtype),
        compiler_params=_params(("arbitrary", "arbitrary"), window),
        name="matmul",
    )(a, w)


def _ffn_up_kernel(h_ref, wg_ref, wu_ref, o_ref):
    h = h_ref[...]
    g = jnp.dot(h, wg_ref[...].astype(BF16), preferred_element_type=F32)
    u = jnp.dot(h, wu_ref[...].astype(BF16), preferred_element_type=F32)
    o_ref[...] = (_silu(g) * u).astype(o_ref.dtype)


def ffn_up(h, w_gu, lead, *, tm, tn):
    m, k = h.shape
    n2 = w_gu.shape[-1]
    n = n2 // 2
    assert m % tm == 0 and n % tn == 0
    nj = n // tn
    nlead = len(lead)
    window = tm * k * 2 + 2 * 2 * k * tn * 4 + 2 * tm * tn * 2 + 2 * k * tn * 2 + 3 * tm * tn * 4
    wblock = (None,) * nlead + (k, tn)
    return pl.pallas_call(
        _ffn_up_kernel,
        grid=(m // tm, nj),
        in_specs=[
            pl.BlockSpec((tm, k), lambda i, j: (i, 0), pipeline_mode=pl.Buffered(1)),
            pl.BlockSpec(wblock, lambda i, j: tuple(lead) + (0, j)),
            pl.BlockSpec(wblock, lambda i, j: tuple(lead) + (0, nj + j)),
        ],
        out_specs=pl.BlockSpec((tm, tn), lambda i, j: (i, j)),
        out_shape=jax.ShapeDtypeStruct((m, n), BF16),
        compiler_params=_params(("arbitrary", "arbitrary"), window),
        name="ffn_up",
    )(h, w_gu, w_gu)


ATTN_BQ = 256


def _rope(x, c, s):
    lane = lax.broadcasted_iota(jnp.int32, x.shape, 1)
    half = AXIS_DIM // 2
    partner = jnp.where(lane % AXIS_DIM < half, pltpu.roll(x, HEAD_DIM - half, 1), pltpu.roll(x, half, 1))
    return x * c + partner * s


def _softmax_pv(s, sink, v):
    m = jnp.maximum(jnp.max(s, axis=-1, keepdims=True), sink)
    e = jnp.exp(s - m)
    denom = jnp.sum(e, axis=-1, keepdims=True) + jnp.exp(sink - m)
    p = (e / denom).astype(BF16)
    return jnp.dot(p, v, preferred_element_type=F32)


def _qk(q, k):
    return lax.dot_general(q, k, (((1,), (1,)), ((), ())), preferred_element_type=F32)


def _attn_kernel(sink_ref, q_ref, kc_ref, vc_ref, kp_ref, vp_ref, kn_ref, vn_ref, kx_ref, vx_ref,
                 cq_ref, sq_ref, cp_ref, sp_ref, cn_ref, sn_ref, o_ref, *, lay):
    i = pl.program_id(0)
    kvh = pl.program_id(1)
    bq, w, hd = ATTN_BQ, ATTN_WINDOW, HEAD_DIM
    past = kx_ref.shape[0]

    @pl.when(i < lay.mc // bq)
    def _context():
        k = kc_ref[...].astype(BF16)
        v = vc_ref[...].astype(BF16)
        for g in range(ATTN_GROUP):
            q = q_ref[:, g * hd:(g + 1) * hd].astype(BF16)
            s = _qk(q, k) * ATTN_SCALE
            o_ref[:, g * hd:(g + 1) * hd] = _softmax_pv(s, sink_ref[kvh * ATTN_GROUP + g], v).astype(o_ref.dtype)

    @pl.when(i >= lay.mc // bq)
    def _latent():
        pos = lax.rem(jnp.maximum(i * bq - lay.mc, 0), lay.ll)
        far = 4 * bq
        no_prev = jnp.where(pos > 0, 0, far)
        no_next = jnp.where(pos + bq < lay.ll, 0, far)
        r_c = lax.broadcasted_iota(jnp.int32, (bq, bq), 0)
        c_c = lax.broadcasted_iota(jnp.int32, (bq, bq), 1)
        r_h = lax.broadcasted_iota(jnp.int32, (bq, w), 0)
        c_h = lax.broadcasted_iota(jnp.int32, (bq, w), 1)
        valid = [jnp.abs(r_c - c_c) <= w,
                 c_h >= r_h + no_prev,
                 c_h + no_next <= r_h - (bq - w)]
        bias = jnp.concatenate([jnp.where(t, 0.0, NEG_INF).astype(F32) for t in valid]
                               + [jnp.zeros((bq, past), F32)], axis=1)
        cq, sq = cq_ref[...], sq_ref[...]
        k = jnp.concatenate([_rope(kc_ref[...], cq, sq), _rope(kp_ref[...], cp_ref[...], sp_ref[...]),
                             _rope(kn_ref[...], cn_ref[...], sn_ref[...]), kx_ref[...]], axis=0).astype(BF16)
        v = jnp.concatenate([vc_ref[...], vp_ref[...], vn_ref[...], vx_ref[...]], axis=0).astype(BF16)
        for g in range(ATTN_GROUP):
            q = _rope(q_ref[:, g * hd:(g + 1) * hd], cq, sq).astype(BF16)
            s = _qk(q, k) * ATTN_SCALE + bias
            o_ref[:, g * hd:(g + 1) * hd] = _softmax_pv(s, sink_ref[kvh * ATTN_GROUP + g], v).astype(o_ref.dtype)


def rope_tables(n_tokens):
    rows = n_tokens // GRID_W
    r = jnp.repeat(jnp.arange(rows), GRID_W).astype(F32)
    col = jnp.tile(jnp.arange(GRID_W), rows).astype(F32)
    inv = ROPE_THETA ** (-jnp.arange(0, AXIS_DIM, 2, dtype=F32) / AXIS_DIM)
    ang = jnp.stack([r[:, None] * inv, col[:, None] * inv], axis=1)
    cos, sin = jnp.cos(ang), jnp.sin(ang)
    c = jnp.stack([cos, cos], axis=2).reshape(n_tokens, HEAD_DIM)
    s = jnp.stack([-sin, sin], axis=2).reshape(n_tokens, HEAD_DIM)
    return c, s


def attention(lay, qkv, sink, cache_k, cache_v, slot, rope_c, rope_s):
    m = qkv.shape[0]
    bq, w, hd = ATTN_BQ, ATTN_WINDOW, HEAD_DIM
    assert lay.lc == bq and lay.ll % bq == 0 and bq % w == 0
    past = cache_k.shape[2]
    qw = ATTN_GROUP * hd
    kcol, vcol = ATTN_NQ // hd, (ATTN_NQ + ATTN_NK) // hd
    per = bq // w

    def pos(i):
        return lax.rem(jnp.maximum(i * bq - lay.mc, 0), lay.ll)

    def prev_row(i):
        return jnp.maximum(i * per - 1, 0)

    def next_row(i):
        return jnp.minimum((i + 1) * per, m // w - 1)

    def cache_map(i, h):
        return (jnp.maximum(i * bq - lay.mc, 0) // lay.ll, slot, 0, h)

    tab_q = pl.BlockSpec((bq, hd), lambda i, h: (pos(i) // bq, 0))
    tab_p = pl.BlockSpec((w, hd), lambda i, h: (jnp.maximum(pos(i) - w, 0) // w, 0))
    tab_n = pl.BlockSpec((w, hd), lambda i, h: (jnp.minimum(pos(i) + bq, lay.ll - w) // w, 0))
    in_specs = [
        pl.BlockSpec(memory_space=pltpu.SMEM),
        pl.BlockSpec((bq, qw), lambda i, h: (i, h)),
        pl.BlockSpec((bq, hd), lambda i, h: (i, kcol + h)),
        pl.BlockSpec((bq, hd), lambda i, h: (i, vcol + h)),
        pl.BlockSpec((w, hd), lambda i, h: (prev_row(i), kcol + h)),
        pl.BlockSpec((w, hd), lambda i, h: (prev_row(i), vcol + h)),
        pl.BlockSpec((w, hd), lambda i, h: (next_row(i), kcol + h)),
        pl.BlockSpec((w, hd), lambda i, h: (next_row(i), vcol + h)),
        pl.BlockSpec((None, None, past, hd), cache_map),
        pl.BlockSpec((None, None, past, hd), cache_map),
        tab_q, tab_q, tab_p, tab_p, tab_n, tab_n,
    ]
    nk = bq + 2 * w + past
    window = 2 * 4 * (bq * qw + 2 * (bq + 2 * w + past) * hd + 6 * bq * hd) + 2 * bq * qw * 2
    work = 4 * bq * nk * 4 + 4 * nk * hd * 2
    return pl.pallas_call(
        functools.partial(_attn_kernel, lay=lay),
        grid=(m // bq, ATTN_KV_HEADS),
        in_specs=in_specs,
        out_specs=pl.BlockSpec((bq, qw), lambda i, h: (i, h)),
        out_shape=jax.ShapeDtypeStruct((m, ATTN_NQ), BF16),
        compiler_params=_params(("arbitrary", "arbitrary"), window + work),
        name="attention",
    )(sink, qkv, qkv, qkv, qkv, qkv, qkv, qkv, cache_k, cache_v, rope_c, rope_s, rope_c, rope_s, rope_c, rope_s)


def _split3(x):
    hi = x.astype(BF16)
    r = x - hi.astype(F32)
    mid = r.astype(BF16)
    lo = (r - mid.astype(F32)).astype(BF16)
    return hi, mid, lo


def _select_rows(sel, x):
    return sum(jnp.dot(sel, p, preferred_element_type=F32) for p in _split3(x))


def _tri(q, fwd):
    r = lax.broadcasted_iota(jnp.int32, (q, q), 0)
    c = lax.broadcasted_iota(jnp.int32, (q, q), 1)
    return (r >= c) if fwd else (r <= c)


CONV_TR = 256
CONV_TC = 1024
CONV_HALO = V7X_SUBLANES


def _ssd_conv_kernel(x_ref, xp_ref, xn_ref, w_ref, b_ref, o_ref, *, lay):
    tr = x_ref.shape[0]
    row0 = pl.program_id(0) * tr
    in_lat = row0 >= lay.mc
    pos = jnp.where(in_lat, lax.rem(jnp.maximum(row0 - lay.mc, 0), lay.ll), lax.rem(row0, lay.lc))
    seq_len = jnp.where(in_lat, lay.ll, lay.lc)
    xp = jnp.where(pos > 0, xp_ref[...], 0.0)
    xn = jnp.where(pos + tr < seq_len, xn_ref[...], 0.0)
    ext = jnp.concatenate([xp, x_ref[...], xn], axis=0)
    n = tr + 2 * CONV_HALO
    acc = x_ref[...] * w_ref[SSD_CONV // 2:SSD_CONV // 2 + 1, :] + b_ref[...]
    for k in range(SSD_CONV):
        d = k - SSD_CONV // 2
        if d != 0:
            tap = pltpu.roll(ext, (n - d) % n, 0)[CONV_HALO:CONV_HALO + tr]
            acc = acc + tap * w_ref[k:k + 1, :]
    o_ref[...] = _silu(acc)


def ssd_conv(lay, proj, conv_w, conv_b, slot):
    m = proj.shape[0]
    tr, tc, halo = CONV_TR, CONV_TC, CONV_HALO
    lay.check_tile(tr)
    assert lay.lc % tr == 0 and SSD_INNER % tc == 0 and SSD_CONV_CH % tc == 0
    col0 = SSD_INNER // tc
    per = tr // halo
    n_slots, _, ch = conv_w.shape
    return pl.pallas_call(
        functools.partial(_ssd_conv_kernel, lay=lay),
        grid=(m // tr, ch // tc),
        in_specs=[
            pl.BlockSpec((tr, tc), lambda i, j: (i, col0 + j)),
            pl.BlockSpec((halo, tc), lambda i, j: (jnp.maximum(i * per - 1, 0), col0 + j)),
            pl.BlockSpec((halo, tc), lambda i, j: (jnp.minimum((i + 1) * per, m // halo - 1), col0 + j)),
            pl.BlockSpec((None, SSD_CONV, tc), lambda i, j: (slot, 0, j)),
            pl.BlockSpec((None, 1, tc), lambda i, j: (slot, 0, j)),
        ],
        out_specs=pl.BlockSpec((tr, tc), lambda i, j: (i, j)),
        out_shape=jax.ShapeDtypeStruct((m, ch), F32),
        compiler_params=_params(("arbitrary", "arbitrary"), 12 * tr * tc * 4),
        name="ssd_conv",
    )(proj, proj, proj, conv_w, conv_b.reshape(n_slots, 1, ch))


SSD_PAIR = 2 * SSD_HEAD_DIM
SSD_GROUP_W = SSD_GROUP_HEADS * SSD_HEAD_DIM


SSD_NG = 4


def _ssd_scan_kernel(*refs, lay, has_h0):
    refs = list(refs)
    dirs = [tuple(refs.pop(0) for _ in range(4)) for _ in range(2)]
    bias_ref, alog_ref = refs.pop(0), refs.pop(0)
    h0_ref = refs.pop(0) if has_h0 else None
    y_refs = (refs.pop(0), refs.pop(0))
    st_ref = refs.pop(0)
    ht_ref, acum_ref, rows_ref, tr_ref = refs
    g0 = pl.program_id(0) * SSD_NG
    q, n_pairs, gw, pw = SSD_CHUNK, SSD_GROUP_W // SSD_PAIR, SSD_GROUP_W, SSD_PAIR
    is_lat, _, c, nc = lay.chunk(pl.program_id(1), q)

    def state_rows(gi, pr):
        return pl.ds(pl.multiple_of(gi * gw + pr * pw, pw), pw)

    @pl.when(c == 0)
    def _init():
        def body(gi, carry):
            for d in range(2):
                for pr in range(n_pairs):
                    if has_h0:
                        init = jnp.where(is_lat, h0_ref[d, state_rows(gi, pr), :].T, 0.0)
                    else:
                        init = jnp.zeros((SSD_STATE, pw), F32)
                    ht_ref[d, gi, :, pr * pw:(pr + 1) * pw] = init
            return carry
        lax.fori_loop(0, SSD_NG, body, 0)

    tris = (_tri(q, True), _tri(q, False))
    first_head = lax.broadcasted_iota(jnp.int32, (SSD_HEADS // 2, pw), 1) < q
    for d in range(2):
        dt = jax.nn.softplus(dirs[d][3][...] + bias_ref[d:d + 1, :])
        a = dt * (-jnp.exp(alog_ref[d:d + 1, :]))
        acum = _select_rows(tris[d].astype(BF16), a)
        acum_ref[d] = acum
        total = acum[q - 1:q, :] if d == 0 else acum[0:1, :]
        to_end = jnp.exp(total - acum) * dt
        for k, v in enumerate((acum, dt, to_end)):
            tr_ref[d, k] = jnp.concatenate([v, v], axis=0).T
            even = tr_ref[d, k, pl.ds(0, SSD_HEADS // 2, stride=2), :]
            odd = tr_ref[d, k, pl.ds(1, SSD_HEADS // 2, stride=2), :]
            rows_ref[d, k] = jnp.where(first_head, even, odd)

    def group_body(gi, carry):
        low = lax.broadcasted_iota(jnp.int32, (q, pw), 1) < SSD_HEAD_DIM
        r2 = lax.broadcasted_iota(jnp.int32, (q, pw), 0)
        c2 = lax.rem(lax.broadcasted_iota(jnp.int32, (q, pw), 1), q)
        tri2 = (r2 >= c2, r2 <= c2)
        same_head = ((lax.broadcasted_iota(jnp.int32, (2 * q, pw), 0) >= q)
                     == (lax.broadcasted_iota(jnp.int32, (2 * q, pw), 1) >= SSD_HEAD_DIM))
        gsel = g0 + gi
        shift = lax.rem(V7X_LANES - gsel * SSD_GROUP_HEADS, V7X_LANES)
        gcol = pl.ds(pl.multiple_of(gi * SSD_STATE, SSD_STATE), SSD_STATE)
        for d, (x_ref, b_ref, c_ref, _) in enumerate(dirs):
            acum_g = pltpu.roll(acum_ref[d], shift, 1)
            b32 = b_ref[:, gcol]
            bm = b32.astype(BF16)
            cm = c_ref[:, gcol].astype(BF16)
            cb2 = _qk(cm, jnp.concatenate([bm, bm], axis=0))
            bt2 = jnp.concatenate([b32, b32], axis=0).T
            for pr in range(n_pairs):
                r0 = 2 * pr
                xcol = state_rows(gi, pr)
                blk = slice(pr * pw, (pr + 1) * pw)
                row = pl.ds(gsel * n_pairs + pr, 1)
                ac = jnp.where(low, acum_g[:, r0:r0 + 1], acum_g[:, r0 + 1:r0 + 2])
                ar, dtr, ter = rows_ref[d, 0, row, :], rows_ref[d, 1, row, :], rows_ref[d, 2, row, :]
                total = ac[q - 1:q, :] if d == 0 else ac[0:1, :]
                w = (cb2 * jnp.exp(jnp.where(tri2[d], ac - ar, -jnp.inf)) * dtr).astype(BF16)
                xp = x_ref[:, xcol]
                xbd = jnp.where(same_head, jnp.concatenate([xp, xp], axis=0), 0.0).astype(BF16)
                ht = ht_ref[d, gi, :, blk]
                y = jnp.dot(w, xbd, preferred_element_type=F32)
                y = y + jnp.dot(cm, ht.astype(BF16), preferred_element_type=F32) * jnp.exp(ac)
                y_refs[d][:, xcol] = y
                bw = (bt2 * ter).astype(BF16)
                ht_ref[d, gi, :, blk] = ht * jnp.exp(total) + jnp.dot(bw, xbd, preferred_element_type=F32)
        return carry

    lax.fori_loop(0, SSD_NG, group_body, 0, unroll=True)

    @pl.when(jnp.logical_and(c == nc - 1, jnp.logical_not(is_lat)))
    def _emit():
        def body(gi, carry):
            for d in range(2):
                for pr in range(n_pairs):
                    st_ref[d, state_rows(gi, pr), :] = ht_ref[d, gi, :, pr * pw:(pr + 1) * pw].T
            return carry
        lax.fori_loop(0, SSD_NG, body, 0)


def ssd_scan(lay, proj, xbc, dt_bias, a_log, slot, h0):
    m = proj.shape[0]
    q, ng = SSD_CHUNK, SSD_NG
    gw, st = SSD_GROUP_W, SSD_STATE
    assert SSD_HEADS == V7X_LANES and SSD_GROUPS % ng == 0 and 2 * q == SSD_PAIR
    b_col, c_col = SSD_INNER // (ng * st), (SSD_INNER + SSD_GN) // (ng * st)
    dt_col = (SSD_INNER + SSD_CONV_CH) // V7X_LANES

    def row_f(t):
        return t

    def row_b(t):
        return lay.mirror(t, q)

    def dir_specs(row, d):
        return [
            pl.BlockSpec((q, ng * gw), lambda g, t: (row(t), g)),
            pl.BlockSpec((q, ng * st), lambda g, t: (row(t), b_col + g)),
            pl.BlockSpec((q, ng * st), lambda g, t: (row(t), c_col + g)),
            pl.BlockSpec((q, V7X_LANES), lambda g, t: (row(t), dt_col + d)),
        ]

    def seq_ctx(t):
        is_lat, seq, _, _ = lay.chunk(t, q)
        return jnp.where(is_lat, lay.bc - 1, seq)

    def seq_lat(t):
        is_lat, seq, _, _ = lay.chunk(t, q)
        return jnp.where(is_lat, seq, 0)

    par_spec = pl.BlockSpec((None, 2, SSD_HEADS), lambda g, t: (slot, 0, 0))
    in_specs = dir_specs(row_f, 0) + dir_specs(row_b, 1) + [par_spec, par_spec]
    args = [xbc, xbc, xbc, proj, xbc, xbc, xbc, proj, dt_bias, a_log]
    has_h0 = h0 is not None
    if has_h0:
        in_specs.append(pl.BlockSpec((None, None, 2, ng * gw, st), lambda g, t: (seq_lat(t), slot, 0, g, 0)))
        args.append(h0)
    y_shape = jax.ShapeDtypeStruct((m, SSD_INNER), F32)
    state_bytes = 2 * ng * gw * st * 4
    window = 2 * 4 * (2 * 2 * q * ng * gw + 4 * q * ng * st) + 5 * state_bytes
    return pl.pallas_call(
        functools.partial(_ssd_scan_kernel, lay=lay, has_h0=has_h0),
        grid=(SSD_GROUPS // ng, lay.n_chunks(q)),
        in_specs=in_specs,
        out_specs=[
            pl.BlockSpec((q, ng * gw), lambda g, t: (row_f(t), g)),
            pl.BlockSpec((q, ng * gw), lambda g, t: (row_b(t), g)),
            pl.BlockSpec((None, 2, ng * gw, st), lambda g, t: (seq_ctx(t), 0, g, 0)),
        ],
        out_shape=[y_shape, y_shape, jax.ShapeDtypeStruct((lay.bc, 2, SSD_INNER, st), F32)],
        scratch_shapes=[pltpu.VMEM((2, ng, st, gw), F32), pltpu.VMEM((2, q, SSD_HEADS), F32),
                        pltpu.VMEM((2, 3, SSD_HEADS // 2, SSD_PAIR), F32),
                        pltpu.VMEM((2, 3, SSD_HEADS, SSD_PAIR), F32)],
        compiler_params=_params(("arbitrary", "arbitrary"), window),
        name="ssd_scan",
    )(*args)


SSD_GATE_TM = 256


def _ssd_gate_kernel(yf_ref, yb_ref, x_ref, z_ref, d_ref, g_ref, o_ref):
    y = (yf_ref[...] + yb_ref[...] + d_ref[...] * x_ref[...]) * _silu(z_ref[...])
    o_ref[...] = _rmsnorm(y, g_ref[...]).astype(o_ref.dtype)


def ssd_gate(lay, y_f, y_b, xbc, proj, d_skip, norm_g, slot):
    m = proj.shape[0]
    tm, gw = SSD_GATE_TM, SSD_INNER // SSD_GROUPS
    n_slots = norm_g.shape[0]
    d_cols = jnp.repeat(d_skip[slot], SSD_HEAD_DIM).reshape(1, SSD_INNER)
    blk = pl.BlockSpec((tm, gw), lambda i, g: (i, g))
    return pl.pallas_call(
        _ssd_gate_kernel,
        grid=(m // tm, SSD_GROUPS),
        in_specs=[blk, blk, blk, blk,
                  pl.BlockSpec((1, gw), lambda i, g: (0, g)),
                  pl.BlockSpec((None, 1, gw), lambda i, g: (slot, 0, g))],
        out_specs=blk,
        out_shape=jax.ShapeDtypeStruct((m, SSD_INNER), BF16),
        compiler_params=_params(("arbitrary", "arbitrary"), 16 * tm * gw * 4),
        name="ssd_gate",
    )(y_f, y_b, xbc, proj, d_cols, norm_g.reshape(n_slots, 1, SSD_INNER))


HGRN_HB = 16
HGRN_W = HGRN_HB * HGRN_DK


def _hgrn_scan_kernel(*refs, lay, layer, has_s0):
    refs = list(refs)
    dirs = [tuple(refs.pop(0) for _ in range(3)) for _ in range(2)]
    lb_ref = refs.pop(0)
    s0_ref = refs.pop(0) if has_s0 else None
    y_refs = (refs.pop(0), refs.pop(0))
    st_ref, s_ref = refs.pop(0), refs.pop(0)
    q = HGRN_CHUNK
    is_lat, _, c, nc = lay.chunk(pl.program_id(1), q)

    dk, dv = HGRN_DK, HGRN_DV

    @pl.when(c == 0)
    def _init():
        for d in range(2):
            for pp in range(HGRN_HB // 2):
                s_ref[d, pp] = jnp.zeros((2 * dv, 2 * dk), F32)
                if has_s0:
                    for h in range(2):
                        s_ref[d, pp, h * dv:(h + 1) * dv, h * dk:(h + 1) * dk] = jnp.where(
                            is_lat, s0_ref[d, 2 * pp + h].T, 0.0)

    logits = lb_ref[...]
    e = jnp.exp(logits - jnp.max(logits, axis=0, keepdims=True))
    p = e / jnp.sum(e, axis=0, keepdims=True)
    lb = jnp.zeros(p.shape[1:], F32)
    for l in range(1, layer + 1):
        lb = lb + p[l]

    for d, (q_ref, v_ref, f_ref) in enumerate(dirs):
        fwd = d == 0
        tri = _tri(q, fwd)
        lbd = lb[d:d + 1, :]
        fr = f_ref[...]
        f = lbd + (1.0 - lbd) * jax.nn.sigmoid(fr)
        k = (1.0 - lbd) * jax.nn.sigmoid(-fr)
        b = _select_rows(tri.astype(BF16), jnp.log(f))
        total = b[q - 1:q, :] if fwd else b[0:1, :]
        qd = (_silu(q_ref[...]) * jnp.exp(b)).astype(BF16)
        kd = (k * jnp.exp(-b)).astype(BF16)
        ks = (k * jnp.exp(total - b)).astype(BF16)
        dec = jnp.exp(total)
        r2 = lax.broadcasted_iota(jnp.int32, (q, 2 * q), 0)
        c2 = lax.rem(lax.broadcasted_iota(jnp.int32, (q, 2 * q), 1), q)
        tri2 = (r2 >= c2) if fwd else (r2 <= c2)
        same_head = ((lax.broadcasted_iota(jnp.int32, (2 * q, 2 * dk), 0) >= q)
                     == (lax.broadcasted_iota(jnp.int32, (2 * q, 2 * dk), 1) >= dk))

        def pair_diag(x):
            return jnp.where(same_head, jnp.concatenate([x, x], axis=0), jnp.zeros((), x.dtype))

        for pp in range(HGRN_HB // 2):
            sl = slice(pp * 2 * dk, (pp + 1) * 2 * dk)
            v_bd = pair_diag(v_ref[:, sl].astype(BF16))
            att = jnp.where(tri2, _qk(qd[:, sl], pair_diag(kd[:, sl])), 0.0).astype(BF16)
            s_prev = s_ref[d, pp]
            o = jnp.dot(att, v_bd, preferred_element_type=F32) + _qk(qd[:, sl], s_prev.astype(BF16))
            y_refs[d][:, sl] = o
            s_ref[d, pp] = s_prev * dec[:, sl] + jnp.dot(v_bd.T, pair_diag(ks[:, sl]), preferred_element_type=F32)

    @pl.when(jnp.logical_and(c == nc - 1, jnp.logical_not(is_lat)))
    def _emit():
        for d in range(2):
            for hh in range(HGRN_HB):
                h = hh % 2
                st_ref[d, hh] = s_ref[d, hh // 2, h * dv:(h + 1) * dv, h * dk:(h + 1) * dk].T


def hgrn_scan(lay, proj, lb_logits, layer, slot, s0):
    m = proj.shape[0]
    q, w = HGRN_CHUNK, HGRN_W
    v_col, f_col = HGRN_HK // w, (HGRN_HK + HGRN_HV) // w
    fb_col = f_col + HGRN_HK // w

    def row_f(t):
        return t

    def row_b(t):
        return lay.mirror(t, q)

    def dir_specs(row, fcol):
        return [
            pl.BlockSpec((q, w), lambda hb, t: (row(t), hb)),
            pl.BlockSpec((q, w), lambda hb, t: (row(t), v_col + hb)),
            pl.BlockSpec((q, w), lambda hb, t: (row(t), fcol + hb)),
        ]

    def seq_ctx(t):
        is_lat, seq, _, _ = lay.chunk(t, q)
        return jnp.where(is_lat, lay.bc - 1, seq)

    def seq_lat(t):
        is_lat, seq, _, _ = lay.chunk(t, q)
        return jnp.where(is_lat, seq, 0)

    in_specs = dir_specs(row_f, f_col) + dir_specs(row_b, fb_col)
    in_specs.append(pl.BlockSpec((DEPTH, 2, w), lambda hb, t: (0, 0, hb)))
    args = [proj] * 6 + [lb_logits]
    has_s0 = s0 is not None
    if has_s0:
        in_specs.append(pl.BlockSpec((None, None, 2, HGRN_HB, HGRN_DK, HGRN_DV), lambda hb, t: (seq_lat(t), slot, 0, hb, 0, 0)))
        args.append(s0)
    y_shape = jax.ShapeDtypeStruct((m, HGRN_HV), F32)
    state_bytes = 2 * HGRN_HB * HGRN_DK * HGRN_DV * 4
    window = 2 * 4 * (8 * q * w) + 5 * state_bytes + 16 * q * w * 4
    return pl.pallas_call(
        functools.partial(_hgrn_scan_kernel, lay=lay, layer=layer, has_s0=has_s0),
        grid=(HGRN_HEADS // HGRN_HB, lay.n_chunks(q)),
        in_specs=in_specs,
        out_specs=[
            pl.BlockSpec((q, w), lambda hb, t: (row_f(t), hb)),
            pl.BlockSpec((q, w), lambda hb, t: (row_b(t), hb)),
            pl.BlockSpec((None, 2, HGRN_HB, HGRN_DK, HGRN_DV), lambda hb, t: (seq_ctx(t), 0, hb, 0, 0)),
        ],
        out_shape=[y_shape, y_shape, jax.ShapeDtypeStruct((lay.bc, 2, HGRN_HEADS, HGRN_DK, HGRN_DV), F32)],
        scratch_shapes=[pltpu.VMEM((2, HGRN_HB // 2, 2 * HGRN_DV, 2 * HGRN_DK), F32)],
        compiler_params=_params(("arbitrary", "arbitrary"), window),
        name="hgrn_scan",
    )(*args)


HGRN_NORM_TM = 256


def _hgrn_norm_kernel(yf_ref, yb_ref, g_ref, ng_ref, o_ref):
    ng = ng_ref[...]
    for hh in range(HGRN_HEADS):
        sl = slice(hh * HGRN_DV, (hh + 1) * HGRN_DV)
        o = yf_ref[:, sl] + yb_ref[:, sl]
        o_ref[:, sl] = (_rmsnorm(o, ng) * _silu(g_ref[:, sl])).astype(o_ref.dtype)


def hgrn_norm(lay, y_f, y_b, proj, norm_g, slot):
    m = proj.shape[0]
    tm = HGRN_NORM_TM
    n_slots = norm_g.shape[0]
    g_col = (3 * HGRN_HK + HGRN_HV) // HGRN_HV
    blk = pl.BlockSpec((tm, HGRN_HV), lambda i: (i, 0))
    return pl.pallas_call(
        _hgrn_norm_kernel,
        grid=(m // tm,),
        in_specs=[blk, blk, pl.BlockSpec((tm, HGRN_HV), lambda i: (i, g_col)),
                  pl.BlockSpec((None, 1, HGRN_DV), lambda i: (slot, 0, 0))],
        out_specs=blk,
        out_shape=jax.ShapeDtypeStruct((m, HGRN_HV), BF16),
        compiler_params=_params(("arbitrary",), 10 * tm * HGRN_HV * 4),
        name="hgrn_norm",
    )(y_f, y_b, proj, norm_g.reshape(n_slots, 1, HGRN_DV))


MM_TM = 2048
DEEP_TM = 1024
FFN_TN = 256
PROJ_TN = 512
SSD_IN_TN = 256


def kernel(x_prompt, x_sample, cache_attn_k, cache_attn_v, state_ssd, state_hgrn, c, c_ctx, mod_w, mod_b, norm_g, ffn_w_gu, ffn_w_down, attn_w_in, attn_sink, attn_w_out, ssd_w_in, ssd_conv_w, ssd_conv_b, ssd_dt_bias, ssd_A_log, ssd_D, ssd_norm_g, ssd_w_out, hgrn_w_in, hgrn_lb_logits, hgrn_norm_g, hgrn_w_out):
    bc, lc, d = x_prompt.shape
    bl, ll, _ = x_sample.shape
    lay = Layout(bc, lc, bl, ll)
    lay.check_tile(MM_TM)
    assert d == D_MODEL and 1 + bl <= MOD_ROWS
    mc = lay.mc
    x = jnp.concatenate([x_prompt.reshape(mc, d), x_sample.reshape(bl * ll, d)], axis=0)

    cond = jnp.concatenate([c_ctx[None, :], c, jnp.zeros((MOD_ROWS - 1 - bl, d), F32)], axis=0)
    mod_rows = modulation_all(cond, mod_w, mod_b).reshape(DEPTH * MOD_ROWS * N_MOD, 1, d)
    norm_rows = norm_g.reshape(DEPTH * 6, 1, d)
    rope_c, rope_s = rope_tables(ll)
    past = cache_attn_k.shape[2]
    cache_k = cache_attn_k.reshape(bl, -1, past, ATTN_NK)
    cache_v = cache_attn_v.reshape(bl, -1, past, ATTN_NK)
    ssd_h0 = state_ssd.reshape(bl, -1, 2, SSD_INNER, SSD_STATE)

    def ffn(h, layer, half):
        a = ffn_up(h, ffn_w_gu, (layer, half), tm=MM_TM, tn=FFN_TN)
        return matmul(a, ffn_w_down, (layer, half), tm=DEEP_TM, tn=FFN_TN)

    new_k, new_v, new_ssd, new_hgrn = [], [], [], []
    _, h = ada_step(lay, x, mod_rows, norm_rows, inn=(0, 0))
    for layer in range(DEPTH):
        x, h = ada_step(lay, x, mod_rows, norm_rows, out=(layer, 0, FFN_RES_W, ffn(h, layer, 0)), inn=(layer, 1))
        kind, slot = layer % N_MIXERS, layer // N_MIXERS
        if kind == 0:
            qkv = matmul(h, attn_w_in, (slot,), tm=MM_TM, tn=PROJ_TN)
            new_k.append(qkv[:mc, ATTN_NQ:ATTN_NQ + ATTN_NK].reshape(bc, lc, ATTN_KV_HEADS, HEAD_DIM))
            new_v.append(qkv[:mc, ATTN_NQ + ATTN_NK:].reshape(bc, lc, ATTN_KV_HEADS, HEAD_DIM))
            o = attention(lay, qkv, attn_sink[slot], cache_k, cache_v, slot, rope_c, rope_s)
            y = matmul(o, attn_w_out, (slot,), tm=MM_TM, tn=PROJ_TN)
        elif kind == 1:
            proj = matmul(h, ssd_w_in, (slot,), tm=MM_TM, tn=SSD_IN_TN)
            xbc = ssd_conv(lay, proj, ssd_conv_w, ssd_conv_b, slot)
            y_f, y_b, st = ssd_scan(lay, proj, xbc, ssd_dt_bias, ssd_A_log, slot, ssd_h0)
            new_ssd.append(st.reshape(bc, 2, SSD_HEADS, SSD_HEAD_DIM, SSD_STATE))
            o = ssd_gate(lay, y_f, y_b, xbc, proj, ssd_D, ssd_norm_g, slot)
            y = matmul(o, ssd_w_out, (slot,), tm=DEEP_TM, tn=FFN_TN)
        else:
            proj = matmul(h, hgrn_w_in, (slot,), tm=MM_TM, tn=PROJ_TN)
            y_f, y_b, st = hgrn_scan(lay, proj, hgrn_lb_logits, layer, slot, state_hgrn)
            new_hgrn.append(st)
            o = hgrn_norm(lay, y_f, y_b, proj, hgrn_norm_g, slot)
            y = matmul(o, hgrn_w_out, (slot,), tm=MM_TM, tn=PROJ_TN)
        x, h = ada_step(lay, x, mod_rows, norm_rows, out=(layer, 1, 1.0, y), inn=(layer, 2))
        nxt = (layer + 1, 0) if layer + 1 < DEPTH else None
        x, h = ada_step(lay, x, mod_rows, norm_rows, out=(layer, 2, FFN_RES_W, ffn(h, layer, 1)), inn=nxt)

    return (x[:mc].reshape(bc, lc, d), x[mc:].reshape(bl, ll, d),
            jnp.stack(new_k, axis=1), jnp.stack(new_v, axis=1),
            jnp.stack(new_ssd, axis=1).astype(state_ssd.dtype), jnp.stack(new_hgrn, axis=1).astype(state_hgrn.dtype))
```
